```python
import jax, jax.numpy as jnp
from jax import lax
import numpy as np

D_MODEL = 1024
BATCH = 4
SEQ = 4096
DEPTH = 1

LRU_WIDTH = D_MODEL
LRU_HEADS = 4
LRU_HEAD_DIM = LRU_WIDTH // LRU_HEADS
CONV_WIDTH = 4
LRU_C = 8.0
SGU_WIDTH = D_MODEL
SGU_GROUPS = 4
SGU_GROUP_DIM = SGU_WIDTH // SGU_GROUPS
CHUNK = 128
D_FF = 4 * D_MODEL
NORM_EPS = 1e-6
LN_EPS = 1e-5
OFF_XA = 0
OFF_GA = OFF_XA + LRU_WIDTH
OFF_U = OFF_GA + LRU_WIDTH
OFF_V = OFF_U + SGU_WIDTH
OFF_MA = OFF_V + SGU_WIDTH
OFF_MB = OFF_MA + D_MODEL
D_IN = OFF_MB + D_MODEL

kernel_name = "hybrid_rglru_sgu_gated_block"


def rms_norm(x, g):
    xf = x.astype(jnp.float32)
    y = xf * lax.rsqrt(jnp.mean(xf * xf, axis=-1, keepdims=True) + NORM_EPS)
    return (y * g.astype(jnp.float32)).astype(x.dtype)


def layer_norm(x, g, b):
    xf = x.astype(jnp.float32)
    mu = jnp.mean(xf, axis=-1, keepdims=True)
    var = jnp.mean(jnp.square(xf - mu), axis=-1, keepdims=True)
    y = (xf - mu) * lax.rsqrt(var + LN_EPS)
    return (y * g.astype(jnp.float32) + b.astype(jnp.float32)).astype(x.dtype)


def causal_depthwise_conv(x, w, b):
    k_w = w.shape[0]
    s = x.shape[1]
    xp = jnp.pad(x, ((0, 0), (k_w - 1, 0), (0, 0)))
    out = b
    for k in range(k_w):
        out = out + xp[:, k_w - 1 - k:k_w - 1 - k + s] * w[k]
    return out


def rg_lru(x, w_r, b_r, w_i, b_i, lam):
    bsz, s, c = x.shape
    xh = x.reshape(bsz, s, LRU_HEADS, LRU_HEAD_DIM)
    r = jax.nn.sigmoid(jnp.einsum('bshi,hij->bshj', xh, w_r) + b_r).reshape(bsz, s, c)
    i = jax.nn.sigmoid(jnp.einsum('bshi,hij->bshj', xh, w_i) + b_i).reshape(bsz, s, c)
    log_a = -LRU_C * r.astype(jnp.float32) * jax.nn.softplus(-lam.astype(jnp.float32))
    a = jnp.exp(log_a)
    mult = jnp.sqrt(-jnp.expm1(2.0 * log_a))
    bx = x.astype(jnp.float32) * i.astype(jnp.float32) * mult

    def combine(left, right):
        a_l, b_l = left
        a_r, b_r2 = right
        return a_l * a_r, a_r * b_l + b_r2

    _, h = lax.associative_scan(combine, (a, bx), axis=1)
    return h.astype(x.dtype)


def chunked_spatial_gating(u, v, ln_g, ln_b, w_s, b_s):
    bsz, s, c = v.shape
    n_chunks = s // CHUNK
    v = layer_norm(v, ln_g, ln_b)
    vc = v.reshape(bsz, n_chunks, CHUNK, SGU_GROUPS, SGU_GROUP_DIM)
    mask = jnp.tril(jnp.ones((CHUNK, CHUNK), dtype=w_s.dtype))
    sp = jnp.einsum('gts,bnsgc->bntgc', w_s * mask, vc) + jnp.transpose(b_s)[:, :, None]
    return u * sp.reshape(bsz, s, c)


def setup_inputs(seed: int = 0) -> dict:
    key = jax.random.key(seed)
    ks = jax.random.split(key, 24)
    f32 = jnp.float32
    nrm = lambda k, shape, scale: jax.random.normal(k, shape, f32) * scale
    x = jax.random.normal(ks[0], (BATCH, SEQ, D_MODEL), f32)
    norm_mix_g = 1.0 + nrm(ks[1], (DEPTH, D_MODEL), 0.02)
    w_in = nrm(ks[2], (DEPTH, D_MODEL, D_IN), D_MODEL ** -0.5)
    conv_w = nrm(ks[3], (DEPTH, CONV_WIDTH, LRU_WIDTH), CONV_WIDTH ** -0.5)
    conv_b = nrm(ks[4], (DEPTH, LRU_WIDTH), 0.01)
    w_rgate = nrm(ks[5], (DEPTH, LRU_HEADS, LRU_HEAD_DIM, LRU_HEAD_DIM), LRU_HEAD_DIM ** -0.5)
    b_rgate = nrm(ks[6], (DEPTH, LRU_HEADS, LRU_HEAD_DIM), 0.01)
    w_igate = nrm(ks[7], (DEPTH, LRU_HEADS, LRU_HEAD_DIM, LRU_HEAD_DIM), LRU_HEAD_DIM ** -0.5)
    b_igate = nrm(ks[8], (DEPTH, LRU_HEADS, LRU_HEAD_DIM), 0.01)
    a_c = jax.random.uniform(ks[9], (DEPTH, LRU_WIDTH), f32, 0.9, 0.999)
    sig = a_c ** (1.0 / LRU_C)
    lru_lambda = jnp.log(sig) - jnp.log1p(-sig)
    w_out_a = nrm(ks[10], (DEPTH, LRU_WIDTH, D_MODEL), LRU_WIDTH ** -0.5)
    sgu_ln_g = 1.0 + nrm(ks[11], (DEPTH, SGU_WIDTH), 0.02)
    sgu_ln_b = nrm(ks[12], (DEPTH, SGU_WIDTH), 0.01)
    sgu_w_s = nrm(ks[13], (DEPTH, SGU_GROUPS, CHUNK, CHUNK), CHUNK ** -0.5)
    sgu_b_s = 1.0 + nrm(ks[14], (DEPTH, SGU_GROUPS, CHUNK), 0.01)
    w_out_b = nrm(ks[15], (DEPTH, SGU_WIDTH, D_MODEL), SGU_WIDTH ** -0.5)
    w_out = nrm(ks[16], (DEPTH, D_MODEL, D_MODEL), D_MODEL ** -0.5)
    norm_mlp_g = 1.0 + nrm(ks[17], (DEPTH, D_MODEL), 0.02)
    w_up = nrm(ks[18], (DEPTH, D_MODEL, D_FF), D_MODEL ** -0.5)
    w_down = nrm(ks[19], (DEPTH, D_FF, D_MODEL), D_FF ** -0.5)
    norm_final_g = 1.0 + nrm(ks[20], (D_MODEL,), 0.02)
    return {"x": x, "norm_mix_g": norm_mix_g, "w_in": w_in, "conv_w": conv_w, "conv_b": conv_b,
            "w_rgate": w_rgate, "b_rgate": b_rgate, "w_igate": w_igate, "b_igate": b_igate,
            "lru_lambda": lru_lambda, "w_out_a": w_out_a, "sgu_ln_g": sgu_ln_g, "sgu_ln_b": sgu_ln_b,
            "sgu_w_s": sgu_w_s, "sgu_b_s": sgu_b_s, "w_out_b": w_out_b, "w_out": w_out,
            "norm_mlp_g": norm_mlp_g, "w_up": w_up, "w_down": w_down, "norm_final_g": norm_final_g}


def reference(x, norm_mix_g, w_in, conv_w, conv_b, w_rgate, b_rgate, w_igate, b_igate,
              lru_lambda, w_out_a, sgu_ln_g, sgu_ln_b, sgu_w_s, sgu_b_s, w_out_b, w_out,
              norm_mlp_g, w_up, w_down, norm_final_g):
    h = x
    for l in range(DEPTH):
        n = rms_norm(h, norm_mix_g[l])
        z = n @ w_in[l]
        xa = z[..., OFF_XA:OFF_GA]
        ga = z[..., OFF_GA:OFF_U]
        ub = z[..., OFF_U:OFF_V]
        vb = z[..., OFF_V:OFF_MA]
        ma = z[..., OFF_MA:OFF_MB]
        mb = z[..., OFF_MB:D_IN]
        xa = causal_depthwise_conv(xa, conv_w[l], conv_b[l])
        ya = rg_lru(xa, w_rgate[l], b_rgate[l], w_igate[l], b_igate[l], lru_lambda[l]) * jax.nn.gelu(ga)
        yb = chunked_spatial_gating(jax.nn.gelu(ub), jax.nn.gelu(vb), sgu_ln_g[l], sgu_ln_b[l],
                                    sgu_w_s[l], sgu_b_s[l])
        merged = jax.nn.sigmoid(ma) * (ya @ w_out_a[l]) + jax.nn.sigmoid(mb) * (yb @ w_out_b[l])
        h = h + merged @ w_out[l]
        n2 = rms_norm(h, norm_mlp_g[l])
        h = h + jnp.square(jax.nn.relu(n2 @ w_up[l])) @ w_down[l]
    return rms_norm(h, norm_final_g)
```

```python
import functools

import jax
import jax.numpy as jnp
from jax import lax
from jax.experimental import pallas as pl
from jax.experimental.pallas import tpu as pltpu

D_MODEL = 1024
LRU_HEADS = 4
LRU_HEAD_DIM = D_MODEL // LRU_HEADS
CONV_WIDTH = 4
LRU_C = 8.0
SGU_GROUPS = 4
SGU_GROUP_DIM = D_MODEL // SGU_GROUPS
CHUNK = 128
D_FF = 4 * D_MODEL
D_IN = 6 * D_MODEL
NORM_EPS = 1e-6
LN_EPS = 1e-5

SUBLANES = 8

MIXER_TILE = 256
FFN_TILE = 512
MIXER_VMEM_BYTES = 56 * 1024 * 1024
FFN_VMEM_BYTES = 48 * 1024 * 1024


def _gelu(x):
    return 0.5 * x * (1.0 + jnp.tanh(0.7978845608028654 * (x + 0.044715 * (x * x * x))))


def _sigmoid(x):
    return 0.5 * (1.0 + jnp.tanh(0.5 * x))


def _rms_norm(x, g):
    ms = jnp.mean(x * x, axis=-1, keepdims=True)
    return x * lax.rsqrt(ms + NORM_EPS) * g


def _bf16_dot(a, w):
    return jnp.dot(a.astype(jnp.bfloat16), w, preferred_element_type=jnp.float32)


def _linear_scan(a, b, carry):
    tm, d = a.shape
    groups = tm // SUBLANES
    a3 = a.reshape(groups, SUBLANES, d)
    b3 = b.reshape(groups, SUBLANES, d)
    row = lax.broadcasted_iota(jnp.int32, (groups, SUBLANES, d), 1)
    shift = 1
    while shift < SUBLANES:
        keep = row >= shift
        a_prev = jnp.where(keep, pltpu.roll(a3, shift, 1), 1.0)
        b_prev = jnp.where(keep, pltpu.roll(b3, shift, 1), 0.0)
        b3 = a3 * b_prev + b3
        a3 = a3 * a_prev
        shift *= 2
    out = []
    for j in range(groups):
        hj = a3[j] * carry + b3[j]
        carry = jnp.broadcast_to(hj[SUBLANES - 1:SUBLANES, :], (SUBLANES, d))
        out.append(hj)
    return jnp.concatenate(out, axis=0), carry


def _mixer_kernel(x_ref, g_ref, w_in_ref, conv_w_ref, conv_b_ref, w_r_ref, b_r_ref,
                  w_i_ref, b_i_ref, lam_ref, w_oa_ref, ln_g_ref, ln_b_ref, w_s_ref,
                  b_s_ref, w_ob_ref, w_o_ref, out_ref, xa_buf, h_carry):
    tm = x_ref.shape[1]
    d = D_MODEL

    @pl.when(pl.program_id(1) == 0)
    def _():
        xa_buf[0:SUBLANES, :] = jnp.zeros((SUBLANES, d), jnp.float32)
        h_carry[...] = jnp.zeros_like(h_carry)

    x = x_ref[0]
    n = _rms_norm(x, g_ref[...]).astype(jnp.bfloat16)

    def proj(k):
        return jnp.dot(n, w_in_ref[:, k * d:(k + 1) * d], preferred_element_type=jnp.float32)

    xa = proj(0)
    xa_buf[SUBLANES:SUBLANES + tm, :] = xa
    xc = conv_b_ref[...] + conv_w_ref[0:1, :] * xa
    for k in range(1, CONV_WIDTH):
        xc = xc + conv_w_ref[k:k + 1, :] * xa_buf[pl.ds(SUBLANES - k, tm), :]
    xa_buf[0:SUBLANES, :] = xa_buf[tm:tm + SUBLANES, :]

    xcb = xc.astype(jnp.bfloat16)
    r_parts, i_parts = [], []
    for h in range(LRU_HEADS):
        xh = xcb[:, h * LRU_HEAD_DIM:(h + 1) * LRU_HEAD_DIM]
        r_parts.append(jnp.dot(xh, w_r_ref[h], preferred_element_type=jnp.float32))
        i_parts.append(jnp.dot(xh, w_i_ref[h], preferred_element_type=jnp.float32))
    r = _sigmoid(jnp.concatenate(r_parts, axis=1) + b_r_ref[...])
    i = _sigmoid(jnp.concatenate(i_parts, axis=1) + b_i_ref[...])

    neg_lam = -lam_ref[...]
    softplus = jnp.maximum(neg_lam, 0.0) + jnp.log1p(jnp.exp(-jnp.abs(neg_lam)))
    log_a = (-LRU_C * softplus) * r
    a = jnp.exp(log_a)
    t = jnp.tanh(log_a)
    mult = jnp.sqrt(-2.0 * t / (1.0 - t))
    bx = xc * i * mult
    h, carry = _linear_scan(a, bx, h_carry[...])
    h_carry[...] = carry
    ya = h * _gelu(proj(1))
    pa = _bf16_dot(ya, w_oa_ref[...])

    u = _gelu(proj(2))
    v = _gelu(proj(3))
    mu = jnp.mean(v, axis=-1, keepdims=True)
    vc = v - mu
    var = jnp.mean(vc * vc, axis=-1, keepdims=True)
    vn = (vc * lax.rsqrt(var + LN_EPS) * ln_g_ref[...] + ln_b_ref[...]).astype(jnp.bfloat16)
    t_idx = lax.broadcasted_iota(jnp.int32, (CHUNK, CHUNK), 0)
    s_idx = lax.broadcasted_iota(jnp.int32, (CHUNK, CHUNK), 1)
    causal = s_idx <= t_idx
    sp_rows = []
    for c in range(tm // CHUNK):
        sp_cols = []
        for g in range(SGU_GROUPS):
            w_g = jnp.where(causal, w_s_ref[g], 0.0).astype(jnp.bfloat16)
            v_cg = vn[c * CHUNK:(c + 1) * CHUNK, g * SGU_GROUP_DIM:(g + 1) * SGU_GROUP_DIM]
            sp = jnp.dot(w_g, v_cg, preferred_element_type=jnp.float32)
            sp_cols.append(sp + b_s_ref[:, g:g + 1])
        sp_rows.append(jnp.concatenate(sp_cols, axis=1))
    yb = u * jnp.concatenate(sp_rows, axis=0)
    pb = _bf16_dot(yb, w_ob_ref[...])

    merged = _sigmoid(proj(4)) * pa + _sigmoid(proj(5)) * pb
    out_ref[0] = x + _bf16_dot(merged, w_o_ref[...])


def _ffn_kernel(h_ref, g_ref, w_up_ref, w_down_ref, gf_ref, out_ref):
    h = h_ref[...]
    n = _rms_norm(h, g_ref[...])
    f = jnp.square(jnp.maximum(_bf16_dot(n, w_up_ref[...]), 0.0))
    h = h + _bf16_dot(f, w_down_ref[...])
    out_ref[...] = _rms_norm(h, gf_ref[...])


def _resident(shape):
    zeros = (0,) * len(shape)
    return pl.BlockSpec(shape, lambda *_: zeros, pipeline_mode=pl.Buffered(1))


def _mixer(x, g, w_in, conv_w, conv_b, w_r, b_r, w_i, b_i, lam, w_oa, ln_g, ln_b, w_s,
           b_s_t, w_ob, w_o):
    bsz, seq, d = x.shape
    tm = MIXER_TILE
    params = (g, w_in, conv_w, conv_b, w_r, b_r, w_i, b_i, lam, w_oa, ln_g, ln_b, w_s,
              b_s_t, w_ob, w_o)
    tile = pl.BlockSpec((1, tm, d), lambda b, s: (b, s, 0))
    return pl.pallas_call(
        _mixer_kernel,
        grid=(bsz, seq // tm),
        in_specs=[tile] + [_resident(p.shape) for p in params],
        out_specs=tile,
        out_shape=jax.ShapeDtypeStruct(x.shape, x.dtype),
        scratch_shapes=[pltpu.VMEM((tm + SUBLANES, d), jnp.float32),
                        pltpu.VMEM((SUBLANES, d), jnp.float32)],
        compiler_params=pltpu.CompilerParams(
            dimension_semantics=("arbitrary", "arbitrary"),
            vmem_limit_bytes=MIXER_VMEM_BYTES),
        name="mixer",
    )(x, *params)


def _ffn(h, g, w_up, w_down, gf):
    t, d = h.shape
    tm = FFN_TILE
    params = (g, w_up, w_down, gf)
    tile = pl.BlockSpec((tm, d), lambda i: (i, 0))
    return pl.pallas_call(
        _ffn_kernel,
        grid=(t // tm,),
        in_specs=[tile] + [_resident(p.shape) for p in params],
        out_specs=tile,
        out_shape=jax.ShapeDtypeStruct(h.shape, h.dtype),
        compiler_params=pltpu.CompilerParams(
            dimension_semantics=("arbitrary",),
            vmem_limit_bytes=FFN_VMEM_BYTES),
        name="ffn",
    )(h, *params)


def kernel(x, norm_mix_g, w_in, conv_w, conv_b, w_rgate, b_rgate, w_igate, b_igate, lru_lambda, w_out_a, sgu_ln_g, sgu_ln_b, sgu_w_s, sgu_b_s, w_out_b, w_out, norm_mlp_g, w_up, w_down, norm_final_g):
    assert w_in.shape[0] == 1
    bsz, seq, d = x.shape
    bf16 = jnp.bfloat16
    row = lambda p: p.reshape(1, -1)
    h = _mixer(
        x, row(norm_mix_g[0]), w_in[0].astype(bf16), conv_w[0], row(conv_b[0]),
        w_rgate[0].astype(bf16), row(b_rgate[0]), w_igate[0].astype(bf16), row(b_igate[0]),
        row(lru_lambda[0]), w_out_a[0].astype(bf16), row(sgu_ln_g[0]), row(sgu_ln_b[0]),
        sgu_w_s[0], jnp.transpose(sgu_b_s[0]), w_out_b[0].astype(bf16), w_out[0].astype(bf16))
    out = _ffn(h.reshape(bsz * seq, d), row(norm_mlp_g[0]), w_up[0].astype(bf16),
               w_down[0].astype(bf16), row(norm_final_g))
    return out.reshape(bsz, seq, d)
```

```python
import functools

import jax
import jax.numpy as jnp
from jax import lax
from jax.experimental import pallas as pl
from jax.experimental.pallas import tpu as pltpu

D_MODEL = 1024
LRU_HEADS = 4
LRU_HEAD_DIM = D_MODEL // LRU_HEADS
CONV_WIDTH = 4
LRU_C = 8.0
SGU_GROUPS = 4
SGU_GROUP_DIM = D_MODEL // SGU_GROUPS
CHUNK = 128
D_FF = 4 * D_MODEL
D_IN = 6 * D_MODEL
NORM_EPS = 1e-6
LN_EPS = 1e-5

SUBLANES = 8

MIXER_TILE = 256
PROJ_SLAB = 512
FFN_TILE = 512
MIXER_VMEM_BYTES = 56 * 1024 * 1024
FFN_VMEM_BYTES = 48 * 1024 * 1024


def _gelu(x):
    return 0.5 * x * (1.0 + jnp.tanh(0.7978845608028654 * (x + 0.044715 * (x * x * x))))


def _sigmoid(x):
    return 0.5 * (1.0 + jnp.tanh(0.5 * x))


def _rms_norm(x, g):
    ms = jnp.mean(x * x, axis=-1, keepdims=True)
    return x * lax.rsqrt(ms + NORM_EPS) * g


def _pack_rows(w):
    *lead, k, n = w.shape
    pairs = w.astype(jnp.bfloat16).reshape(*lead, k // 2, 2, n)
    return lax.bitcast_convert_type(jnp.swapaxes(pairs, -1, -2), jnp.uint32)


def _packed_dot(a, w_packed):
    w = pltpu.bitcast(w_packed, jnp.bfloat16)
    return jnp.dot(a.astype(jnp.bfloat16), w, preferred_element_type=jnp.float32)


def _linear_scan(a, b, carry, tick):
    tm, d = a.shape
    groups = tm // SUBLANES
    a3 = a.reshape(groups, SUBLANES, d)
    b3 = b.reshape(groups, SUBLANES, d)
    row = lax.broadcasted_iota(jnp.int32, (groups, SUBLANES, d), 1)
    shift = 1
    while shift < SUBLANES:
        keep = row >= shift
        a_prev = jnp.where(keep, pltpu.roll(a3, shift, 1), 1.0)
        b_prev = jnp.where(keep, pltpu.roll(b3, shift, 1), 0.0)
        b3 = a3 * b_prev + b3
        a3 = a3 * a_prev
        shift *= 2
        tick(1)
    out = []
    for j in range(groups):
        hj = a3[j] * carry + b3[j]
        carry = jnp.broadcast_to(hj[SUBLANES - 1:SUBLANES, :], (SUBLANES, d))
        out.append(hj)
    return jnp.concatenate(out, axis=0), carry


def _project_slabs(x_ref, g_ref, w_in_ref, z_ref):
    n = _rms_norm(x_ref[...], g_ref[...]).astype(jnp.bfloat16)
    for k in range(D_IN // PROJ_SLAB):
        cols = slice(k * PROJ_SLAB, (k + 1) * PROJ_SLAB)
        z_ref[:, cols] = _packed_dot(n, w_in_ref[:, cols])
        yield


def _mix(z_ref, x_ref, first_of_batch, conv_w_ref, conv_b_ref, w_r_ref, b_r_ref, w_i_ref, b_i_ref,
         lam_ref, w_oa_ref, ln_g_ref, ln_b_ref, w_s_ref, b_s_ref, w_ob_ref, w_o_ref, out_ref,
         xa_buf, h_carry, tick):
    tm = x_ref.shape[0]
    d = D_MODEL

    def zcol(k):
        return z_ref[:, k * d:(k + 1) * d]

    xa = zcol(0)
    xa_buf[0:SUBLANES, :] = jnp.where(first_of_batch, 0.0, xa_buf[tm:tm + SUBLANES, :])
    xa_buf[SUBLANES:SUBLANES + tm, :] = xa
    xc = conv_b_ref[...] + conv_w_ref[0:1, :] * xa
    for k in range(1, CONV_WIDTH):
        xc = xc + conv_w_ref[k:k + 1, :] * xa_buf[pl.ds(SUBLANES - k, tm), :]

    tick(2)

    xcb = xc.astype(jnp.bfloat16)
    r_parts, i_parts = [], []
    for h in range(LRU_HEADS):
        xh = xcb[:, h * LRU_HEAD_DIM:(h + 1) * LRU_HEAD_DIM]
        r_parts.append(_packed_dot(xh, w_r_ref[h]))
        i_parts.append(_packed_dot(xh, w_i_ref[h]))
    r = _sigmoid(jnp.concatenate(r_parts, axis=1) + b_r_ref[...])
    i = _sigmoid(jnp.concatenate(i_parts, axis=1) + b_i_ref[...])

    neg_lam = -lam_ref[...]
    softplus = jnp.maximum(neg_lam, 0.0) + jnp.log1p(jnp.exp(-jnp.abs(neg_lam)))
    log_a = (-LRU_C * softplus) * r
    a = jnp.exp(log_a)
    t = jnp.tanh(log_a)
    mult = jnp.sqrt(-2.0 * t / (1.0 - t))
    bx = xc * i * mult
    tick(2)
    h, carry = _linear_scan(a, bx, jnp.where(first_of_batch, 0.0, h_carry[...]), tick)
    h_carry[...] = carry
    ya = h * _gelu(zcol(1))
    tick(1)
    pa = _packed_dot(ya, w_oa_ref[...])

    u = _gelu(zcol(2))
    tick(1)
    v = _gelu(zcol(3))
    tick(1)
    mu = jnp.mean(v, axis=-1, keepdims=True)
    vc = v - mu
    var = jnp.mean(vc * vc, axis=-1, keepdims=True)
    vn = (vc * lax.rsqrt(var + LN_EPS) * ln_g_ref[...] + ln_b_ref[...]).astype(jnp.bfloat16)
    tick(1)
    t_idx = lax.broadcasted_iota(jnp.int32, (CHUNK, CHUNK), 0)
    s_idx = lax.broadcasted_iota(jnp.int32, (CHUNK, CHUNK), 1)
    causal = s_idx <= t_idx
    sp_rows = []
    for c in range(tm // CHUNK):
        sp_cols = []
        for g in range(SGU_GROUPS):
            w_g = jnp.where(causal, w_s_ref[g], 0.0).astype(jnp.bfloat16)
            v_cg = vn[c * CHUNK:(c + 1) * CHUNK, g * SGU_GROUP_DIM:(g + 1) * SGU_GROUP_DIM]
            sp = jnp.dot(w_g, v_cg, preferred_element_type=jnp.float32)
            sp_cols.append(sp + b_s_ref[:, g:g + 1])
        sp_rows.append(jnp.concatenate(sp_cols, axis=1))
    yb = u * jnp.concatenate(sp_rows, axis=0)
    pb = _packed_dot(yb, w_ob_ref[...])

    merged = _sigmoid(zcol(4)) * pa + _sigmoid(zcol(5)) * pb
    tick(1)
    out_ref[...] = x_ref[...] + _packed_dot(merged, w_o_ref[...])


def _mixer_kernel(tiles_per_batch, x_ref, xprev_ref, g_ref, w_in_ref, *rest):
    *mix_params, out_ref, z_even, z_odd, xa_buf, h_carry = rest
    step = pl.program_id(0)
    first_of_batch = (step - 1) % tiles_per_batch == 0

    @pl.when(step == 0)
    def _():
        z_odd[...] = jnp.zeros_like(z_odd)
        xa_buf[...] = jnp.zeros_like(xa_buf)
        h_carry[...] = jnp.zeros_like(h_carry)

    def body(z_new, z_old):
        slabs = _project_slabs(x_ref, g_ref, w_in_ref, z_new)

        def tick(n):
            for _ in range(n):
                next(slabs, None)

        _mix(z_old, xprev_ref, first_of_batch, *mix_params, out_ref, xa_buf, h_carry, tick)
        for _ in slabs:
            pass

    @pl.when(step % 2 == 0)
    def _():
        body(z_even, z_odd)

    @pl.when(step % 2 == 1)
    def _():
        body(z_odd, z_even)


def _ffn_kernel(h_ref, g_ref, w_up_ref, w_down_ref, gf_ref, out_ref):
    h = h_ref[...]
    n = _rms_norm(h, g_ref[...])
    f = jnp.square(jnp.maximum(_packed_dot(n, w_up_ref[...]), 0.0))
    h = h + _packed_dot(f, w_down_ref[...])
    out_ref[...] = _rms_norm(h, gf_ref[...])


def _resident(shape):
    zeros = (0,) * len(shape)
    return pl.BlockSpec(shape, lambda *_: zeros, pipeline_mode=pl.Buffered(1))


def _mixer(x, g, w_in, *mix_params):
    t, d = x.shape[0] * x.shape[1], x.shape[2]
    tm = MIXER_TILE
    n_tiles = t // tm
    x2 = x.reshape(t, d)
    params = (g, w_in) + mix_params
    cur_tile = pl.BlockSpec((tm, d), lambda i: (jnp.minimum(i, n_tiles - 1), 0))
    prev_tile = pl.BlockSpec((tm, d), lambda i: (jnp.maximum(i - 1, 0), 0))
    return pl.pallas_call(
        functools.partial(_mixer_kernel, x.shape[1] // tm),
        grid=(n_tiles + 1,),
        in_specs=[cur_tile, prev_tile] + [_resident(p.shape) for p in params],
        out_specs=prev_tile,
        out_shape=jax.ShapeDtypeStruct(x2.shape, x2.dtype),
        scratch_shapes=[pltpu.VMEM((tm, D_IN), jnp.float32),
                        pltpu.VMEM((tm, D_IN), jnp.float32),
                        pltpu.VMEM((tm + SUBLANES, d), jnp.float32),
                        pltpu.VMEM((SUBLANES, d), jnp.float32)],
        compiler_params=pltpu.CompilerParams(
            dimension_semantics=("arbitrary",),
            vmem_limit_bytes=MIXER_VMEM_BYTES),
        name="mixer",
    )(x2, x2, *params)


def _ffn(h, g, w_up, w_down, gf):
    t, d = h.shape
    tm = FFN_TILE
    params = (g, w_up, w_down, gf)
    tile = pl.BlockSpec((tm, d), lambda i: (i, 0))
    return pl.pallas_call(
        _ffn_kernel,
        grid=(t // tm,),
        in_specs=[tile] + [_resident(p.shape) for p in params],
        out_specs=tile,
        out_shape=jax.ShapeDtypeStruct(h.shape, h.dtype),
        compiler_params=pltpu.CompilerParams(
            dimension_semantics=("arbitrary",),
            vmem_limit_bytes=FFN_VMEM_BYTES),
        name="ffn",
    )(h, *params)


def kernel(x, norm_mix_g, w_in, conv_w, conv_b, w_rgate, b_rgate, w_igate, b_igate, lru_lambda, w_out_a, sgu_ln_g, sgu_ln_b, sgu_w_s, sgu_b_s, w_out_b, w_out, norm_mlp_g, w_up, w_down, norm_final_g):
    assert w_in.shape[0] == 1
    bsz, seq, d = x.shape
    row = lambda p: p.reshape(1, -1)
    h = _mixer(
        x, row(norm_mix_g[0]), _pack_rows(w_in[0]), conv_w[0], row(conv_b[0]),
        _pack_rows(w_rgate[0]), row(b_rgate[0]), _pack_rows(w_igate[0]), row(b_igate[0]),
        row(lru_lambda[0]), _pack_rows(w_out_a[0]), row(sgu_ln_g[0]), row(sgu_ln_b[0]),
        sgu_w_s[0], jnp.transpose(sgu_b_s[0]), _pack_rows(w_out_b[0]), _pack_rows(w_out[0]))
    out = _ffn(h, row(norm_mlp_g[0]), _pack_rows(w_up[0]), _pack_rows(w_down[0]),
               row(norm_final_g))
    return out.reshape(bsz, seq, d)
```

```python
import functools

import jax
import jax.numpy as jnp
from jax import lax
from jax.experimental import pallas as pl
from jax.experimental.pallas import tpu as pltpu

D_MODEL = 1024
LRU_HEADS = 4
LRU_HEAD_DIM = D_MODEL // LRU_HEADS
CONV_WIDTH = 4
LRU_C = 8.0
SGU_GROUPS = 4
SGU_GROUP_DIM = D_MODEL // SGU_GROUPS
CHUNK = 128
D_FF = 4 * D_MODEL
D_IN = 6 * D_MODEL
NORM_EPS = 1e-6
LN_EPS = 1e-5

SUBLANES = 8

MIXER_TILE = 256
SEG = MIXER_TILE // SUBLANES
CONV_TAIL = (CONV_WIDTH - 1) * SUBLANES
PROJ_SLAB = 512
FFN_TILE = 512
MIXER_VMEM_BYTES = 56 * 1024 * 1024
FFN_VMEM_BYTES = 48 * 1024 * 1024

_GELU_C1 = -2.0 * 0.7978845608028654
_GELU_C2 = _GELU_C1 * 0.044715


def _gelu(x):
    return x / (1.0 + jnp.exp(x * (_GELU_C1 + _GELU_C2 * (x * x))))


def _sigmoid(x):
    return 1.0 / (1.0 + jnp.exp(-x))


def _rms_norm(x, g):
    ms = jnp.mean(x * x, axis=-1, keepdims=True)
    return x * lax.rsqrt(ms + NORM_EPS) * g


def _bf16_dot(a, w):
    return jnp.dot(a.astype(jnp.bfloat16), w, preferred_element_type=jnp.float32)


def _segment_time(p):
    return (p % SUBLANES) * SEG + p // SUBLANES


def _sublane_shift(x, prev):
    sub = lax.broadcasted_iota(jnp.int32, x.shape, 0) % SUBLANES
    groups = x.shape[0] // SUBLANES
    x3 = x.reshape(groups, SUBLANES, x.shape[1])
    p3 = prev.reshape(groups, SUBLANES, x.shape[1])
    return jnp.where(sub == 0, pltpu.roll(p3, 1, 1).reshape(x.shape),
                     pltpu.roll(x3, 1, 1).reshape(x.shape))


def _segment_scan(a, b, carry, tick):
    tm, d = a.shape
    grp = lambda v, j: v[j * SUBLANES:(j + 1) * SUBLANES, :]
    hs, prods = [grp(b, 0)], [grp(a, 0)]
    for j in range(1, SEG):
        hs.append(grp(a, j) * hs[-1] + grp(b, j))
        prods.append(grp(a, j) * prods[-1])
        if j % (SEG // 2) == 0:
            tick(1)
    a_tot, h_end = prods[-1], hs[-1]
    sub = lax.broadcasted_iota(jnp.int32, (SUBLANES, d), 0)
    shift = 1
    while shift < SUBLANES:
        keep = sub >= shift
        a_prev = jnp.where(keep, pltpu.roll(a_tot, shift, 0), 1.0)
        h_prev = jnp.where(keep, pltpu.roll(h_end, shift, 0), 0.0)
        h_end = a_tot * h_prev + h_end
        a_tot = a_tot * a_prev
        shift *= 2
    seg_end = a_tot * carry + h_end
    start = jnp.where(sub == 0, carry, pltpu.roll(seg_end, 1, 0))
    new_carry = jnp.broadcast_to(seg_end[SUBLANES - 1:SUBLANES, :], (SUBLANES, d))
    h = jnp.concatenate([hs[j] + prods[j] * start for j in range(SEG)], axis=0)
    return h, new_carry


def _project_slabs(x_ref, g_ref, w_in_ref, perm_ref, z_ref):
    n = _rms_norm(x_ref[...], g_ref[...]).astype(jnp.bfloat16)
    n = jnp.dot(perm_ref[...], n, preferred_element_type=jnp.float32).astype(jnp.bfloat16)
    for k in range(D_IN // PROJ_SLAB):
        cols = slice(k * PROJ_SLAB, (k + 1) * PROJ_SLAB)
        z_ref[:, cols] = jnp.dot(n, w_in_ref[:, cols], preferred_element_type=jnp.float32)
        yield


def _mix(z_ref, x_ref, first_of_batch, conv_w_ref, conv_b_ref, w_r_ref, b_r_ref, w_i_ref, b_i_ref,
         lam_ref, w_oa_ref, ln_g_ref, ln_b_ref, b_s_ref, w_ob_ref, w_o_ref, out_ref,
         unperm_ref, w_sp_ref, tail_buf, h_carry, tick):
    tm = x_ref.shape[0]
    d = D_MODEL

    def zcol(k):
        return z_ref[:, k * d:(k + 1) * d]

    xa = zcol(0)
    tail = xa[tm - CONV_TAIL:, :]
    wrapped = _sublane_shift(tail, jnp.where(first_of_batch, 0.0, tail_buf[...]))
    tail_buf[...] = tail
    xc = conv_b_ref[...] + conv_w_ref[0:1, :] * xa
    for k in range(1, CONV_WIDTH):
        back = k * SUBLANES
        x_prev = jnp.concatenate([wrapped[CONV_TAIL - back:, :], xa[:tm - back, :]], axis=0)
        xc = xc + conv_w_ref[k:k + 1, :] * x_prev
    tick(2)

    xcb = xc.astype(jnp.bfloat16)
    r_parts, i_parts = [], []
    for h in range(LRU_HEADS):
        xh = xcb[:, h * LRU_HEAD_DIM:(h + 1) * LRU_HEAD_DIM]
        r_parts.append(jnp.dot(xh, w_r_ref[h], preferred_element_type=jnp.float32))
        i_parts.append(jnp.dot(xh, w_i_ref[h], preferred_element_type=jnp.float32))
    r = _sigmoid(jnp.concatenate(r_parts, axis=1) + b_r_ref[...])
    i = _sigmoid(jnp.concatenate(i_parts, axis=1) + b_i_ref[...])

    neg_lam = -lam_ref[...]
    softplus = jnp.maximum(neg_lam, 0.0) + jnp.log1p(jnp.exp(-jnp.abs(neg_lam)))
    log_a = (-LRU_C * softplus) * r
    a = jnp.exp(log_a)
    t = jnp.tanh(log_a)
    mult = jnp.sqrt(-2.0 * t / (1.0 - t))
    bx = xc * i * mult
    tick(2)
    h, carry = _segment_scan(a, bx, jnp.where(first_of_batch, 0.0, h_carry[...]), tick)
    h_carry[...] = carry
    tick(1)
    ya = h * _gelu(zcol(1))
    tick(1)
    pa = _bf16_dot(ya, w_oa_ref[...])

    u = _gelu(zcol(2))
    tick(1)
    v = _gelu(zcol(3))
    tick(1)
    mu = jnp.mean(v, axis=-1, keepdims=True)
    vc = v - mu
    var = jnp.mean(vc * vc, axis=-1, keepdims=True)
    vn = (vc * lax.rsqrt(var + LN_EPS) * ln_g_ref[...] + ln_b_ref[...]).astype(jnp.bfloat16)
    tick(1)
    sp_cols = []
    for g in range(SGU_GROUPS):
        v_g = vn[:, g * SGU_GROUP_DIM:(g + 1) * SGU_GROUP_DIM]
        sp = jnp.dot(w_sp_ref[g], v_g, preferred_element_type=jnp.float32)
        sp_cols.append(sp + b_s_ref[:, g:g + 1])
    yb = u * jnp.concatenate(sp_cols, axis=1)
    pb = _bf16_dot(yb, w_ob_ref[...])

    merged = (_sigmoid(zcol(4)) * pa + _sigmoid(zcol(5)) * pb).astype(jnp.bfloat16)
    tick(1)
    merged = jnp.dot(unperm_ref[...], merged, preferred_element_type=jnp.float32)
    out_ref[...] = x_ref[...] + _bf16_dot(merged, w_o_ref[...])


def _setup_permutations(w_s_ref, perm_ref, unperm_ref, w_sp_ref):
    tm = perm_ref.shape[0]
    rows = lax.broadcasted_iota(jnp.int32, (tm, tm), 0)
    cols = lax.broadcasted_iota(jnp.int32, (tm, tm), 1)
    perm = (cols == _segment_time(rows)).astype(jnp.bfloat16)
    unperm = (rows == _segment_time(cols)).astype(jnp.bfloat16)
    perm_ref[...] = perm
    unperm_ref[...] = unperm
    mixes_chunk = (rows // CHUNK == cols // CHUNK) & (cols <= rows)
    reps = tm // CHUNK
    for g in range(SGU_GROUPS):
        w_tiled = jnp.concatenate([jnp.concatenate([w_s_ref[g]] * reps, axis=1)] * reps, axis=0)
        w_time = jnp.where(mixes_chunk, w_tiled, 0.0).astype(jnp.bfloat16)
        w_rows = jnp.dot(perm, w_time, preferred_element_type=jnp.float32).astype(jnp.bfloat16)
        w_sp_ref[g] = jnp.dot(w_rows, unperm, preferred_element_type=jnp.float32).astype(jnp.bfloat16)


def _mixer_kernel(tiles_per_batch, x_ref, xprev_ref, g_ref, w_in_ref, *rest):
    (conv_w_ref, conv_b_ref, w_r_ref, b_r_ref, w_i_ref, b_i_ref, lam_ref, w_oa_ref, ln_g_ref,
     ln_b_ref, w_s_ref, b_s_ref, w_ob_ref, w_o_ref, out_ref,
     z_even, z_odd, perm_ref, unperm_ref, w_sp_ref, tail_buf, h_carry) = rest
    step = pl.program_id(0)
    first_of_batch = (step - 1) % tiles_per_batch == 0

    @pl.when(step == 0)
    def _():
        _setup_permutations(w_s_ref, perm_ref, unperm_ref, w_sp_ref)
        z_odd[...] = jnp.zeros_like(z_odd)
        tail_buf[...] = jnp.zeros_like(tail_buf)
        h_carry[...] = jnp.zeros_like(h_carry)

    def body(z_new, z_old):
        slabs = _project_slabs(x_ref, g_ref, w_in_ref, perm_ref, z_new)

        def tick(n):
            for _ in range(n):
                next(slabs, None)

        _mix(z_old, xprev_ref, first_of_batch, conv_w_ref, conv_b_ref, w_r_ref, b_r_ref, w_i_ref,
             b_i_ref, lam_ref, w_oa_ref, ln_g_ref, ln_b_ref, b_s_ref, w_ob_ref, w_o_ref, out_ref,
             unperm_ref, w_sp_ref, tail_buf, h_carry, tick)
        for _ in slabs:
            pass

    @pl.when(step % 2 == 0)
    def _():
        body(z_even, z_odd)

    @pl.when(step % 2 == 1)
    def _():
        body(z_odd, z_even)


def _ffn_kernel(h_ref, g_ref, w_up_ref, w_down_ref, gf_ref, out_ref):
    h = h_ref[...]
    n = _rms_norm(h, g_ref[...])
    f = jnp.square(jnp.maximum(_bf16_dot(n, w_up_ref[...]), 0.0))
    h = h + _bf16_dot(f, w_down_ref[...])
    out_ref[...] = _rms_norm(h, gf_ref[...])


def _resident(shape):
    zeros = (0,) * len(shape)
    return pl.BlockSpec(shape, lambda *_: zeros, pipeline_mode=pl.Buffered(1))


def _mixer(x, *params):
    t, d = x.shape[0] * x.shape[1], x.shape[2]
    tm = MIXER_TILE
    n_tiles = t // tm
    x2 = x.reshape(t, d)
    cur_tile = pl.BlockSpec((tm, d), lambda i: (jnp.minimum(i, n_tiles - 1), 0))
    prev_tile = pl.BlockSpec((tm, d), lambda i: (jnp.maximum(i - 1, 0), 0))
    return pl.pallas_call(
        functools.partial(_mixer_kernel, x.shape[1] // tm),
        grid=(n_tiles + 1,),
        in_specs=[cur_tile, prev_tile] + [_resident(p.shape) for p in params],
        out_specs=prev_tile,
        out_shape=jax.ShapeDtypeStruct(x2.shape, x2.dtype),
        scratch_shapes=[pltpu.VMEM((tm, D_IN), jnp.float32),
                        pltpu.VMEM((tm, D_IN), jnp.float32),
                        pltpu.VMEM((tm, tm), jnp.bfloat16),
                        pltpu.VMEM((tm, tm), jnp.bfloat16),
                        pltpu.VMEM((SGU_GROUPS, tm, tm), jnp.bfloat16),
                        pltpu.VMEM((CONV_TAIL, d), jnp.float32),
                        pltpu.VMEM((SUBLANES, d), jnp.float32)],
        compiler_params=pltpu.CompilerParams(
            dimension_semantics=("arbitrary",),
            vmem_limit_bytes=MIXER_VMEM_BYTES),
        name="mixer",
    )(x2, x2, *params)


def _ffn(h, g, w_up, w_down, gf):
    t, d = h.shape
    tm = FFN_TILE
    params = (g, w_up, w_down, gf)
    tile = pl.BlockSpec((tm, d), lambda i: (i, 0))
    return pl.pallas_call(
        _ffn_kernel,
        grid=(t // tm,),
        in_specs=[tile] + [_resident(p.shape) for p in params],
        out_specs=tile,
        out_shape=jax.ShapeDtypeStruct(h.shape, h.dtype),
        compiler_params=pltpu.CompilerParams(
            dimension_semantics=("arbitrary",),
            vmem_limit_bytes=FFN_VMEM_BYTES),
        name="ffn",
    )(h, *params)


def _segment_major_bias(b_s):
    per_time = jnp.tile(jnp.transpose(b_s), (MIXER_TILE // CHUNK, 1))
    return per_time.reshape(SUBLANES, SEG, -1).swapaxes(0, 1).reshape(MIXER_TILE, -1)


def kernel(x, norm_mix_g, w_in, conv_w, conv_b, w_rgate, b_rgate, w_igate, b_igate, lru_lambda, w_out_a, sgu_ln_g, sgu_ln_b, sgu_w_s, sgu_b_s, w_out_b, w_out, norm_mlp_g, w_up, w_down, norm_final_g):
    assert w_in.shape[0] == 1
    bsz, seq, d = x.shape
    bf16 = jnp.bfloat16
    row = lambda p: p.reshape(1, -1)
    h = _mixer(
        x, row(norm_mix_g[0]), w_in[0].astype(bf16), conv_w[0], row(conv_b[0]),
        w_rgate[0].astype(bf16), row(b_rgate[0]), w_igate[0].astype(bf16), row(b_igate[0]),
        row(lru_lambda[0]), w_out_a[0].astype(bf16), row(sgu_ln_g[0]), row(sgu_ln_b[0]),
        sgu_w_s[0], _segment_major_bias(sgu_b_s[0]), w_out_b[0].astype(bf16),
        w_out[0].astype(bf16))
    out = _ffn(h, row(norm_mlp_g[0]), w_up[0].astype(bf16), w_down[0].astype(bf16),
               row(norm_final_g))
    return out.reshape(bsz, seq, d)
```

```python
import functools
import math

import jax
import jax.numpy as jnp
from jax import lax
from jax.experimental import pallas as pl
from jax.experimental.pallas import tpu as pltpu

D_MODEL = 1024
LRU_HEADS = 4
LRU_HEAD_DIM = D_MODEL // LRU_HEADS
CONV_WIDTH = 4
LRU_C = 8.0
SGU_GROUPS = 4
SGU_GROUP_DIM = D_MODEL // SGU_GROUPS
CHUNK = 128
D_FF = 4 * D_MODEL
D_IN = 6 * D_MODEL
NORM_EPS = 1e-6
LN_EPS = 1e-5

SUBLANES = 8
LANES = 128
BF16_ROWS = 16

MIXER_TILE = 256
SEG = MIXER_TILE // SUBLANES
CONV_TAIL = (CONV_WIDTH - 1) * SUBLANES
ROW_BLOCK = BF16_ROWS
PROJ_SLAB = 256
FFN_TILE = 512
MIXER_VMEM_BYTES = 56 * 1024 * 1024
FFN_VMEM_BYTES = 48 * 1024 * 1024

_COST_CONV, _COST_GATE, _COST_SCAN, _COST_FIX = 9, 14, 3, 9
_COST_GELU_LN, _COST_GATED, _COST_MERGE = 20, 4, 8
_COST_TOTAL = (_COST_CONV + _COST_GATE + _COST_SCAN + _COST_FIX + _COST_GELU_LN + _COST_GATED
               + _COST_MERGE) * (MIXER_TILE // ROW_BLOCK)
SLAB_EVERY = 0.3 * _COST_TOTAL / (D_IN // PROJ_SLAB)

_LOG2E = math.log2(math.e)
_GELU_C1 = -2.0 * 0.7978845608028654 * _LOG2E
_GELU_C2 = _GELU_C1 * 0.044715


def _gelu(x):
    return x / (1.0 + jnp.exp2(x * (_GELU_C1 + _GELU_C2 * (x * x))))


def _sigmoid(x):
    return 1.0 / (1.0 + jnp.exp2(-_LOG2E * x))


def _rms_norm(x, g):
    ms = jnp.mean(x * x, axis=-1, keepdims=True)
    return x * lax.rsqrt(ms + NORM_EPS) * g


def _bf16_dot(a, w):
    return jnp.dot(a.astype(jnp.bfloat16), w, preferred_element_type=jnp.float32)


def _segment_time(p):
    return (p % SUBLANES) * SEG + p // SUBLANES


def _sublane_shift(x, prev):
    sub = lax.broadcasted_iota(jnp.int32, x.shape, 0) % SUBLANES
    groups = x.shape[0] // SUBLANES
    x3 = x.reshape(groups, SUBLANES, x.shape[1])
    p3 = prev.reshape(groups, SUBLANES, x.shape[1])
    return jnp.where(sub == 0, pltpu.roll(p3, 1, 1).reshape(x.shape),
                     pltpu.roll(x3, 1, 1).reshape(x.shape))


def _row_blocks(tm):
    return [slice(r, r + ROW_BLOCK) for r in range(0, tm, ROW_BLOCK)]


def _once_complete(lhs_ref, last_block, zero_ref):
    bits = pltpu.bitcast(last_block[0:SUBLANES, 0:LANES], jnp.uint32) & zero_ref[...]
    zero = pltpu.bitcast(bits, jnp.float32)
    zero = jnp.concatenate([zero] * (BF16_ROWS // SUBLANES), axis=0).astype(lhs_ref.dtype)
    lhs = lhs_ref[...]
    corner = lhs[0:BF16_ROWS, 0:LANES] + zero
    top = jnp.concatenate([corner, lhs[0:BF16_ROWS, LANES:]], axis=1)
    return jnp.concatenate([top, lhs[BF16_ROWS:, :]], axis=0)


def _normalise(x_ref, g_ref, perm_ref, n_buf, n_perm):
    g = g_ref[...]
    for rows in _row_blocks(x_ref.shape[0]):
        n_buf[rows, :] = _rms_norm(x_ref[rows, :], g).astype(jnp.bfloat16)
    n_perm[...] = jnp.dot(perm_ref[...], n_buf[...],
                          preferred_element_type=jnp.float32).astype(jnp.bfloat16)


def _project_slabs(n_perm, w_in_ref, z_ref):
    for k in range(D_IN // PROJ_SLAB):
        cols = slice(k * PROJ_SLAB, (k + 1) * PROJ_SLAB)
        z_ref[:, cols] = jnp.dot(n_perm[...], w_in_ref[:, cols], preferred_element_type=jnp.float32)
        yield


def _mix(z_ref, x_ref, first_of_batch, conv_w_ref, conv_b_ref, w_r_ref, b_r_ref, w_i_ref, b_i_ref,
         lam_ref, w_oa_ref, ln_g_ref, ln_b_ref, b_s_ref, w_ob_ref, w_o_ref, out_ref,
         unperm_ref, w_sp_ref, zero_ref, tail_buf, h_carry, f32_a, f32_b, f32_c, f32_d, f32_e,
         bf_a, bf_b, spent):
    tm = x_ref.shape[0]
    d = D_MODEL
    blocks = _row_blocks(tm)
    col = lambda k: slice(k * d, (k + 1) * d)

    tail = z_ref[tm - CONV_TAIL:tm, col(0)]
    wrapped = _sublane_shift(tail, jnp.where(first_of_batch, 0.0, tail_buf[...]))
    tail_buf[...] = tail
    conv_b = conv_b_ref[...]
    conv_w = [conv_w_ref[k:k + 1, :] for k in range(CONV_WIDTH)]
    for h in range(LRU_HEADS):
        hcols = slice(h * LRU_HEAD_DIM, (h + 1) * LRU_HEAD_DIM)
        for rows in blocks:
            xc = conv_b[:, hcols] + conv_w[0][:, hcols] * z_ref[rows, hcols]
            for k in range(1, CONV_WIDTH):
                lo = rows.start - k * SUBLANES
                pieces = []
                if lo < 0:
                    pieces.append(
                        wrapped[CONV_TAIL + lo:min(CONV_TAIL, CONV_TAIL + lo + ROW_BLOCK), hcols])
                if lo + ROW_BLOCK > 0:
                    pieces.append(z_ref[max(lo, 0):lo + ROW_BLOCK, hcols])
                xc = xc + conv_w[k][:, hcols] * jnp.concatenate(pieces, axis=0)
            f32_c[rows, hcols] = xc
            bf_a[rows, hcols] = xc.astype(jnp.bfloat16)
            spent(_COST_CONV / LRU_HEADS)
        xcb = bf_a[:, hcols]
        f32_a[:, hcols] = jnp.dot(xcb, w_r_ref[h], preferred_element_type=jnp.float32)
        f32_b[:, hcols] = jnp.dot(xcb, w_i_ref[h], preferred_element_type=jnp.float32)

    ln_g, ln_b = ln_g_ref[...], ln_b_ref[...]
    for rows in blocks:
        f32_d[rows, :] = _gelu(z_ref[rows, col(2)])
        v = _gelu(z_ref[rows, col(3)])
        vc = v - jnp.mean(v, axis=-1, keepdims=True)
        var = jnp.mean(vc * vc, axis=-1, keepdims=True)
        vn = vc * lax.rsqrt(var + LN_EPS) * ln_g + ln_b
        bf_b[rows, :] = vn.astype(jnp.bfloat16)
        spent(_COST_GELU_LN)

    vnb = _once_complete(bf_b, vn, zero_ref)
    for g in range(SGU_GROUPS):
        gcols = slice(g * SGU_GROUP_DIM, (g + 1) * SGU_GROUP_DIM)
        f32_e[:, gcols] = jnp.dot(w_sp_ref[g], vnb[:, gcols], preferred_element_type=jnp.float32)

    neg_lam = -lam_ref[...]
    softplus = jnp.maximum(neg_lam, 0.0) + jnp.log1p(jnp.exp(-jnp.abs(neg_lam)))
    log_a_scale = -LRU_C * softplus
    b_r, b_i = b_r_ref[...], b_i_ref[...]
    for rows in blocks:
        log_a = log_a_scale * _sigmoid(f32_a[rows, :] + b_r)
        t = jnp.tanh(log_a)
        q = -2.0 * t / (1.0 - t)
        mult = jnp.where(q > 0.0, q * lax.rsqrt(q), 0.0)
        f32_b[rows, :] = f32_c[rows, :] * _sigmoid(f32_b[rows, :] + b_i) * mult
        f32_a[rows, :] = jnp.exp(log_a)
        spent(_COST_GATE)

    for rows in blocks:
        bias = jnp.concatenate(
            [jnp.broadcast_to(b_s_ref[rows, g:g + 1], (ROW_BLOCK, SGU_GROUP_DIM))
             for g in range(SGU_GROUPS)], axis=1)
        yb = f32_d[rows, :] * (f32_e[rows, :] + bias)
        bf_a[rows, :] = yb.astype(jnp.bfloat16)
        spent(_COST_GATED)
    f32_c[...] = jnp.dot(_once_complete(bf_a, yb, zero_ref), w_ob_ref[...],
                         preferred_element_type=jnp.float32)

    grp = lambda j: slice(j * SUBLANES, (j + 1) * SUBLANES)
    h_run, a_run = f32_b[grp(0), :], f32_a[grp(0), :]
    for j in range(1, SEG):
        a_j = f32_a[grp(j), :]
        h_run = a_j * h_run + f32_b[grp(j), :]
        a_run = a_j * a_run
        f32_b[grp(j), :] = h_run
        f32_a[grp(j), :] = a_run
        if j % (ROW_BLOCK // SUBLANES) == 0:
            spent(_COST_SCAN)
    sub = lax.broadcasted_iota(jnp.int32, (SUBLANES, d), 0)
    shift = 1
    while shift < SUBLANES:
        keep = sub >= shift
        a_prev = jnp.where(keep, pltpu.roll(a_run, shift, 0), 1.0)
        h_prev = jnp.where(keep, pltpu.roll(h_run, shift, 0), 0.0)
        h_run = a_run * h_prev + h_run
        a_run = a_run * a_prev
        shift *= 2
    carry = jnp.where(first_of_batch, 0.0, h_carry[...])
    seg_end = a_run * carry + h_run
    seg_start = jnp.where(sub == 0, carry, pltpu.roll(seg_end, 1, 0))
    h_carry[...] = jnp.broadcast_to(seg_end[SUBLANES - 1:SUBLANES, :], (SUBLANES, d))
    seg_start = jnp.concatenate([seg_start] * (ROW_BLOCK // SUBLANES), axis=0)
    for rows in blocks:
        h = f32_b[rows, :] + f32_a[rows, :] * seg_start
        ya = h * _gelu(z_ref[rows, col(1)])
        bf_b[rows, :] = ya.astype(jnp.bfloat16)
        spent(_COST_FIX)
    f32_e[...] = jnp.dot(_once_complete(bf_b, ya, zero_ref), w_oa_ref[...],
                         preferred_element_type=jnp.float32)

    for rows in blocks:
        merged = (_sigmoid(z_ref[rows, col(4)]) * f32_e[rows, :]
                  + _sigmoid(z_ref[rows, col(5)]) * f32_c[rows, :])
        bf_a[rows, :] = merged.astype(jnp.bfloat16)
        spent(_COST_MERGE)
    merged = jnp.dot(unperm_ref[...], _once_complete(bf_a, merged, zero_ref),
                     preferred_element_type=jnp.float32)
    out_ref[...] = x_ref[...] + _bf16_dot(merged, w_o_ref[...])


def _setup_permutations(w_s_ref, perm_ref, unperm_ref, w_sp_ref):
    tm = perm_ref.shape[0]
    rows = lax.broadcasted_iota(jnp.int32, (tm, tm), 0)
    cols = lax.broadcasted_iota(jnp.int32, (tm, tm), 1)
    perm = (cols == _segment_time(rows)).astype(jnp.bfloat16)
    unperm = (rows == _segment_time(cols)).astype(jnp.bfloat16)
    perm_ref[...] = perm
    unperm_ref[...] = unperm
    mixes_chunk = (rows // CHUNK == cols // CHUNK) & (cols <= rows)
    reps = tm // CHUNK
    for g in range(SGU_GROUPS):
        w_tiled = jnp.concatenate([jnp.concatenate([w_s_ref[g]] * reps, axis=1)] * reps, axis=0)
        w_time = jnp.where(mixes_chunk, w_tiled, 0.0).astype(jnp.bfloat16)
        w_rows = jnp.dot(perm, w_time, preferred_element_type=jnp.float32).astype(jnp.bfloat16)
        w_sp_ref[g] = jnp.dot(w_rows, unperm, preferred_element_type=jnp.float32).astype(jnp.bfloat16)


def _mixer_kernel(tiles_per_batch, xnext_ref, xprev_ref, g_ref, w_in_ref, *rest):
    (conv_w_ref, conv_b_ref, w_r_ref, b_r_ref, w_i_ref, b_i_ref, lam_ref, w_oa_ref, ln_g_ref,
     ln_b_ref, w_s_ref, b_s_ref, w_ob_ref, w_o_ref, out_ref,
     z_even, z_odd, perm_ref, unperm_ref, w_sp_ref, zero_ref, tail_buf, h_carry,
     n_buf, n_perm, f32_a, f32_b, f32_c, f32_d, f32_e, bf_a, bf_b) = rest
    step = pl.program_id(0)
    first_of_batch = (step - 1) % tiles_per_batch == 0

    @pl.when(step == 0)
    def _():
        _setup_permutations(w_s_ref, perm_ref, unperm_ref, w_sp_ref)
        z_odd[...] = jnp.zeros_like(z_odd)
        zero_ref[...] = jnp.zeros_like(zero_ref)
        tail_buf[...] = jnp.zeros_like(tail_buf)
        h_carry[...] = jnp.zeros_like(h_carry)
        _normalise(xprev_ref, g_ref, perm_ref, n_buf, n_perm)

    def body(z_new, z_old):
        slabs = _project_slabs(n_perm, w_in_ref, z_new)
        meter = [0.0, 0.0]

        def spent(cost):
            meter[0] += cost
            while meter[0] >= meter[1]:
                meter[1] += SLAB_EVERY
                next(slabs, None)

        _mix(z_old, xprev_ref, first_of_batch, conv_w_ref, conv_b_ref, w_r_ref, b_r_ref, w_i_ref,
             b_i_ref, lam_ref, w_oa_ref, ln_g_ref, ln_b_ref, b_s_ref, w_ob_ref, w_o_ref, out_ref,
             unperm_ref, w_sp_ref, zero_ref, tail_buf, h_carry, f32_a, f32_b, f32_c, f32_d, f32_e,
             bf_a, bf_b, spent)
        for _ in slabs:
            pass
        _normalise(xnext_ref, g_ref, perm_ref, n_buf, n_perm)

    @pl.when(step % 2 == 0)
    def _():
        body(z_even, z_odd)

    @pl.when(step % 2 == 1)
    def _():
        body(z_odd, z_even)


def _ffn_kernel(h_ref, g_ref, w_up_ref, w_down_ref, gf_ref, out_ref):
    h = h_ref[...]
    n = _rms_norm(h, g_ref[...])
    f = jnp.square(jnp.maximum(_bf16_dot(n, w_up_ref[...]), 0.0))
    h = h + _bf16_dot(f, w_down_ref[...])
    out_ref[...] = _rms_norm(h, gf_ref[...])


def _resident(shape):
    zeros = (0,) * len(shape)
    return pl.BlockSpec(shape, lambda *_: zeros, pipeline_mode=pl.Buffered(1))


def _mixer(x, *params):
    t, d = x.shape[0] * x.shape[1], x.shape[2]
    tm = MIXER_TILE
    n_tiles = t // tm
    x2 = x.reshape(t, d)
    next_tile = pl.BlockSpec((tm, d), lambda i: (jnp.minimum(i + 1, n_tiles - 1), 0))
    prev_tile = pl.BlockSpec((tm, d), lambda i: (jnp.maximum(i - 1, 0), 0))
    f32, bf16 = jnp.float32, jnp.bfloat16
    return pl.pallas_call(
        functools.partial(_mixer_kernel, x.shape[1] // tm),
        grid=(n_tiles + 1,),
        in_specs=[next_tile, prev_tile] + [_resident(p.shape) for p in params],
        out_specs=prev_tile,
        out_shape=jax.ShapeDtypeStruct(x2.shape, x2.dtype),
        scratch_shapes=[pltpu.VMEM((tm, D_IN), f32),
                        pltpu.VMEM((tm, D_IN), f32),
                        pltpu.VMEM((tm, tm), bf16),
                        pltpu.VMEM((tm, tm), bf16),
                        pltpu.VMEM((SGU_GROUPS, tm, tm), bf16),
                        pltpu.VMEM((SUBLANES, LANES), jnp.uint32),
                        pltpu.VMEM((CONV_TAIL, d), f32),
                        pltpu.VMEM((SUBLANES, d), f32),
                        pltpu.VMEM((tm, d), bf16),
                        pltpu.VMEM((tm, d), bf16)]
                       + [pltpu.VMEM((tm, d), f32)] * 5
                       + [pltpu.VMEM((tm, d), bf16)] * 2,
        compiler_params=pltpu.CompilerParams(
            dimension_semantics=("arbitrary",),
            vmem_limit_bytes=MIXER_VMEM_BYTES),
        name="mixer",
    )(x2, x2, *params)


def _ffn(h, g, w_up, w_down, gf):
    t, d = h.shape
    tm = FFN_TILE
    params = (g, w_up, w_down, gf)
    tile = pl.BlockSpec((tm, d), lambda i: (i, 0))
    return pl.pallas_call(
        _ffn_kernel,
        grid=(t // tm,),
        in_specs=[tile] + [_resident(p.shape) for p in params],
        out_specs=tile,
        out_shape=jax.ShapeDtypeStruct(h.shape, h.dtype),
        compiler_params=pltpu.CompilerParams(
            dimension_semantics=("arbitrary",),
            vmem_limit_bytes=FFN_VMEM_BYTES),
        name="ffn",
    )(h, *params)


def _segment_major_bias(b_s):
    per_time = jnp.tile(jnp.transpose(b_s), (MIXER_TILE // CHUNK, 1))
    return per_time.reshape(SUBLANES, SEG, -1).swapaxes(0, 1).reshape(MIXER_TILE, -1)


def kernel(x, norm_mix_g, w_in, conv_w, conv_b, w_rgate, b_rgate, w_igate, b_igate, lru_lambda, w_out_a, sgu_ln_g, sgu_ln_b, sgu_w_s, sgu_b_s, w_out_b, w_out, norm_mlp_g, w_up, w_down, norm_final_g):
    assert w_in.shape[0] == 1
    bsz, seq, d = x.shape
    bf16 = jnp.bfloat16
    row = lambda p: p.reshape(1, -1)
    h = _mixer(
        x, row(norm_mix_g[0]), w_in[0].astype(bf16), conv_w[0], row(conv_b[0]),
        w_rgate[0].astype(bf16), row(b_rgate[0]), w_igate[0].astype(bf16), row(b_igate[0]),
        row(lru_lambda[0]), w_out_a[0].astype(bf16), row(sgu_ln_g[0]), row(sgu_ln_b[0]),
        sgu_w_s[0], _segment_major_bias(sgu_b_s[0]), w_out_b[0].astype(bf16),
        w_out[0].astype(bf16))
    out = _ffn(h, row(norm_mlp_g[0]), w_up[0].astype(bf16), w_down[0].astype(bf16),
               row(norm_final_g))
    return out.reshape(bsz, seq, d)
```

```python
import functools
import math

import jax
import jax.numpy as jnp
from jax import lax
from jax.experimental import pallas as pl
from jax.experimental.pallas import tpu as pltpu

D_MODEL = 1024
LRU_HEADS = 4
LRU_HEAD_DIM = D_MODEL // LRU_HEADS
CONV_WIDTH = 4
LRU_C = 8.0
SGU_GROUPS = 4
SGU_GROUP_DIM = D_MODEL // SGU_GROUPS
CHUNK = 128
D_FF = 4 * D_MODEL
D_IN = 6 * D_MODEL
NORM_EPS = 1e-6
LN_EPS = 1e-5

SUBLANES = 8
LANES = 128
BF16_ROWS = 16

MIXER_TILE = 256
SEG = MIXER_TILE // SUBLANES
CONV_TAIL = (CONV_WIDTH - 1) * SUBLANES
ROW_BLOCK = BF16_ROWS
PROJ_SLAB = 256
FFN_TILE = 512
STAGE_ROWS = 256
MIXER_VMEM_BYTES = 56 * 1024 * 1024
FFN_VMEM_BYTES = 48 * 1024 * 1024

_COST_CONV, _COST_GATE, _COST_SCAN, _COST_FIX = 9, 14, 3, 9
_COST_GELU_LN, _COST_GATED, _COST_MERGE = 20, 4, 8
_COST_TOTAL = (_COST_CONV + _COST_GATE + _COST_SCAN + _COST_FIX + _COST_GELU_LN + _COST_GATED
               + _COST_MERGE) * (MIXER_TILE // ROW_BLOCK)
SLAB_EVERY = 0.3 * _COST_TOTAL / (D_IN // PROJ_SLAB)

_LOG2E = math.log2(math.e)
_GELU_C1 = -2.0 * 0.7978845608028654 * _LOG2E
_GELU_C2 = _GELU_C1 * 0.044715


def _gelu(x):
    return x / (1.0 + jnp.exp2(x * (_GELU_C1 + _GELU_C2 * (x * x))))


def _sigmoid(x):
    return 1.0 / (1.0 + jnp.exp2(-_LOG2E * x))


def _rms_norm(x, g):
    ms = jnp.mean(x * x, axis=-1, keepdims=True)
    return x * lax.rsqrt(ms + NORM_EPS) * g


def _bf16_dot(a, w):
    return jnp.dot(a.astype(jnp.bfloat16), w, preferred_element_type=jnp.float32)


def _segment_time(p):
    return (p % SUBLANES) * SEG + p // SUBLANES


def _sublane_shift(x, prev):
    sub = lax.broadcasted_iota(jnp.int32, x.shape, 0) % SUBLANES
    groups = x.shape[0] // SUBLANES
    x3 = x.reshape(groups, SUBLANES, x.shape[1])
    p3 = prev.reshape(groups, SUBLANES, x.shape[1])
    return jnp.where(sub == 0, pltpu.roll(p3, 1, 1).reshape(x.shape),
                     pltpu.roll(x3, 1, 1).reshape(x.shape))


def _row_blocks(tm):
    return [slice(r, r + ROW_BLOCK) for r in range(0, tm, ROW_BLOCK)]


def _once_complete(lhs_ref, last_block, zero_ref):
    bits = pltpu.bitcast(last_block[0:SUBLANES, 0:LANES], jnp.uint32) & zero_ref[...]
    zero = pltpu.bitcast(bits, jnp.float32)
    zero = jnp.concatenate([zero] * (BF16_ROWS // SUBLANES), axis=0).astype(lhs_ref.dtype)
    lhs = lhs_ref[...]
    corner = lhs[0:BF16_ROWS, 0:LANES] + zero
    top = jnp.concatenate([corner, lhs[0:BF16_ROWS, LANES:]], axis=1)
    return jnp.concatenate([top, lhs[BF16_ROWS:, :]], axis=0)


def _load_as_bf16(src_hbm, dst, stage, sems):
    rows, width = stage.shape[1], stage.shape[2]
    k, n = src_hbm.shape
    assert k % rows == 0 and n % width == 0
    chunks = [(r, c) for c in range(0, n, width) for r in range(0, k, rows)]

    def copy(i):
        r, c = chunks[i]
        return pltpu.make_async_copy(
            src_hbm.at[pl.ds(r, rows), pl.ds(c, width)], stage.at[i % 2], sems.at[i % 2])

    copy(0).start()
    for i, (r, c) in enumerate(chunks):
        if i + 1 < len(chunks):
            copy(i + 1).start()
        copy(i).wait()
        dst[r:r + rows, c:c + width] = stage[i % 2].astype(jnp.bfloat16)


def _normalise(x_ref, g_ref, perm_ref, n_buf, n_perm):
    g = g_ref[...]
    for rows in _row_blocks(x_ref.shape[0]):
        n_buf[rows, :] = _rms_norm(x_ref[rows, :], g).astype(jnp.bfloat16)
    n_perm[...] = jnp.dot(perm_ref[...], n_buf[...],
                          preferred_element_type=jnp.float32).astype(jnp.bfloat16)


def _project_slabs(n_perm, w_in_ref, z_ref):
    for k in range(D_IN // PROJ_SLAB):
        cols = slice(k * PROJ_SLAB, (k + 1) * PROJ_SLAB)
        z_ref[:, cols] = jnp.dot(n_perm[...], w_in_ref[:, cols], preferred_element_type=jnp.float32)
        yield


def _mix(z_ref, x_ref, first_of_batch, conv_w_ref, conv_b_ref, w_r_ref, b_r_ref, w_i_ref, b_i_ref,
         lam_ref, w_oa_ref, ln_g_ref, ln_b_ref, b_s_ref, w_ob_ref, w_o_ref, out_ref,
         unperm_ref, w_sp_ref, zero_ref, tail_buf, h_carry, f32_a, f32_b, f32_c, f32_d, f32_e,
         bf_a, bf_b, spent):
    tm = x_ref.shape[0]
    d = D_MODEL
    blocks = _row_blocks(tm)
    col = lambda k: slice(k * d, (k + 1) * d)

    tail = z_ref[tm - CONV_TAIL:tm, col(0)]
    wrapped = _sublane_shift(tail, jnp.where(first_of_batch, 0.0, tail_buf[...]))
    tail_buf[...] = tail
    conv_b = conv_b_ref[...]
    conv_w = [conv_w_ref[k:k + 1, :] for k in range(CONV_WIDTH)]
    for h in range(LRU_HEADS):
        hcols = slice(h * LRU_HEAD_DIM, (h + 1) * LRU_HEAD_DIM)
        for rows in blocks:
            xc = conv_b[:, hcols] + conv_w[0][:, hcols] * z_ref[rows, hcols]
            for k in range(1, CONV_WIDTH):
                lo = rows.start - k * SUBLANES
                pieces = []
                if lo < 0:
                    pieces.append(
                        wrapped[CONV_TAIL + lo:min(CONV_TAIL, CONV_TAIL + lo + ROW_BLOCK), hcols])
                if lo + ROW_BLOCK > 0:
                    pieces.append(z_ref[max(lo, 0):lo + ROW_BLOCK, hcols])
                xc = xc + conv_w[k][:, hcols] * jnp.concatenate(pieces, axis=0)
            f32_c[rows, hcols] = xc
            bf_a[rows, hcols] = xc.astype(jnp.bfloat16)
            spent(_COST_CONV / LRU_HEADS)
        xcb = bf_a[:, hcols]
        f32_a[:, hcols] = jnp.dot(xcb, w_r_ref[hcols, :], preferred_element_type=jnp.float32)
        f32_b[:, hcols] = jnp.dot(xcb, w_i_ref[hcols, :], preferred_element_type=jnp.float32)

    ln_g, ln_b = ln_g_ref[...], ln_b_ref[...]
    for rows in blocks:
        f32_d[rows, :] = _gelu(z_ref[rows, col(2)])
        v = _gelu(z_ref[rows, col(3)])
        vc = v - jnp.mean(v, axis=-1, keepdims=True)
        var = jnp.mean(vc * vc, axis=-1, keepdims=True)
        vn = vc * lax.rsqrt(var + LN_EPS) * ln_g + ln_b
        bf_b[rows, :] = vn.astype(jnp.bfloat16)
        spent(_COST_GELU_LN)

    vnb = _once_complete(bf_b, vn, zero_ref)
    for g in range(SGU_GROUPS):
        gcols = slice(g * SGU_GROUP_DIM, (g + 1) * SGU_GROUP_DIM)
        f32_e[:, gcols] = jnp.dot(w_sp_ref[g], vnb[:, gcols], preferred_element_type=jnp.float32)

    neg_lam = -lam_ref[...]
    softplus = jnp.maximum(neg_lam, 0.0) + jnp.log1p(jnp.exp(-jnp.abs(neg_lam)))
    log_a_scale = -LRU_C * softplus
    b_r, b_i = b_r_ref[...], b_i_ref[...]
    for rows in blocks:
        log_a = log_a_scale * _sigmoid(f32_a[rows, :] + b_r)
        t = jnp.tanh(log_a)
        q = -2.0 * t / (1.0 - t)
        mult = jnp.where(q > 0.0, q * lax.rsqrt(q), 0.0)
        f32_b[rows, :] = f32_c[rows, :] * _sigmoid(f32_b[rows, :] + b_i) * mult
        f32_a[rows, :] = jnp.exp(log_a)
        spent(_COST_GATE)

    for rows in blocks:
        bias = jnp.concatenate(
            [jnp.broadcast_to(b_s_ref[rows, g:g + 1], (ROW_BLOCK, SGU_GROUP_DIM))
             for g in range(SGU_GROUPS)], axis=1)
        yb = f32_d[rows, :] * (f32_e[rows, :] + bias)
        bf_a[rows, :] = yb.astype(jnp.bfloat16)
        spent(_COST_GATED)
    f32_c[...] = jnp.dot(_once_complete(bf_a, yb, zero_ref), w_ob_ref[...],
                         preferred_element_type=jnp.float32)

    grp = lambda j: slice(j * SUBLANES, (j + 1) * SUBLANES)
    h_run, a_run = f32_b[grp(0), :], f32_a[grp(0), :]
    for j in range(1, SEG):
        a_j = f32_a[grp(j), :]
        h_run = a_j * h_run + f32_b[grp(j), :]
        a_run = a_j * a_run
        f32_b[grp(j), :] = h_run
        f32_a[grp(j), :] = a_run
        if j % (ROW_BLOCK // SUBLANES) == 0:
            spent(_COST_SCAN)
    sub = lax.broadcasted_iota(jnp.int32, (SUBLANES, d), 0)
    shift = 1
    while shift < SUBLANES:
        keep = sub >= shift
        a_prev = jnp.where(keep, pltpu.roll(a_run, shift, 0), 1.0)
        h_prev = jnp.where(keep, pltpu.roll(h_run, shift, 0), 0.0)
        h_run = a_run * h_prev + h_run
        a_run = a_run * a_prev
        shift *= 2
    carry = jnp.where(first_of_batch, 0.0, h_carry[...])
    seg_end = a_run * carry + h_run
    seg_start = jnp.where(sub == 0, carry, pltpu.roll(seg_end, 1, 0))
    h_carry[...] = jnp.broadcast_to(seg_end[SUBLANES - 1:SUBLANES, :], (SUBLANES, d))
    seg_start = jnp.concatenate([seg_start] * (ROW_BLOCK // SUBLANES), axis=0)
    for rows in blocks:
        h = f32_b[rows, :] + f32_a[rows, :] * seg_start
        ya = h * _gelu(z_ref[rows, col(1)])
        bf_b[rows, :] = ya.astype(jnp.bfloat16)
        spent(_COST_FIX)
    f32_e[...] = jnp.dot(_once_complete(bf_b, ya, zero_ref), w_oa_ref[...],
                         preferred_element_type=jnp.float32)

    for rows in blocks:
        merged = (_sigmoid(z_ref[rows, col(4)]) * f32_e[rows, :]
                  + _sigmoid(z_ref[rows, col(5)]) * f32_c[rows, :])
        bf_a[rows, :] = merged.astype(jnp.bfloat16)
        spent(_COST_MERGE)
    merged = jnp.dot(unperm_ref[...], _once_complete(bf_a, merged, zero_ref),
                     preferred_element_type=jnp.float32)
    out_ref[...] = x_ref[...] + _bf16_dot(merged, w_o_ref[...])


def _setup_permutations(w_s_ref, perm_ref, unperm_ref, w_sp_ref):
    tm = perm_ref.shape[0]
    rows = lax.broadcasted_iota(jnp.int32, (tm, tm), 0)
    cols = lax.broadcasted_iota(jnp.int32, (tm, tm), 1)
    perm = (cols == _segment_time(rows)).astype(jnp.bfloat16)
    unperm = (rows == _segment_time(cols)).astype(jnp.bfloat16)
    perm_ref[...] = perm
    unperm_ref[...] = unperm
    mixes_chunk = (rows // CHUNK == cols // CHUNK) & (cols <= rows)
    reps = tm // CHUNK
    for g in range(SGU_GROUPS):
        w_tiled = jnp.concatenate([jnp.concatenate([w_s_ref[g]] * reps, axis=1)] * reps, axis=0)
        w_time = jnp.where(mixes_chunk, w_tiled, 0.0).astype(jnp.bfloat16)
        w_rows = jnp.dot(perm, w_time, preferred_element_type=jnp.float32).astype(jnp.bfloat16)
        w_sp_ref[g] = jnp.dot(w_rows, unperm, preferred_element_type=jnp.float32).astype(jnp.bfloat16)


def _mixer_kernel(tiles_per_batch, xnext_ref, xprev_ref, g_ref, w_in_hbm, *rest):
    (conv_w_ref, conv_b_ref, w_r_hbm, b_r_ref, w_i_hbm, b_i_ref, lam_ref, w_oa_hbm, ln_g_ref,
     ln_b_ref, w_s_ref, b_s_ref, w_ob_hbm, w_o_hbm, out_ref,
     z_even, z_odd, perm_ref, unperm_ref, w_sp_ref, zero_ref, tail_buf, h_carry,
     n_buf, n_perm, f32_a, f32_b, f32_c, f32_d, f32_e, bf_a, bf_b,
     w_in_ref, w_r_ref, w_i_ref, w_oa_ref, w_ob_ref, w_o_ref, stage_wide, stage_narrow,
     sems) = rest
    step = pl.program_id(0)
    first_of_batch = (step - 1) % tiles_per_batch == 0

    @pl.when(step == 0)
    def _():
        _load_as_bf16(w_in_hbm, w_in_ref, stage_wide, sems)
        _load_as_bf16(w_r_hbm, w_r_ref, stage_narrow, sems)
        _load_as_bf16(w_i_hbm, w_i_ref, stage_narrow, sems)
        _load_as_bf16(w_oa_hbm, w_oa_ref, stage_wide, sems)
        _load_as_bf16(w_ob_hbm, w_ob_ref, stage_wide, sems)
        _load_as_bf16(w_o_hbm, w_o_ref, stage_wide, sems)
        _setup_permutations(w_s_ref, perm_ref, unperm_ref, w_sp_ref)
        z_odd[...] = jnp.zeros_like(z_odd)
        zero_ref[...] = jnp.zeros_like(zero_ref)
        tail_buf[...] = jnp.zeros_like(tail_buf)
        h_carry[...] = jnp.zeros_like(h_carry)
        _normalise(xprev_ref, g_ref, perm_ref, n_buf, n_perm)

    def body(z_new, z_old):
        slabs = _project_slabs(n_perm, w_in_ref, z_new)
        meter = [0.0, 0.0]

        def spent(cost):
            meter[0] += cost
            while meter[0] >= meter[1]:
                meter[1] += SLAB_EVERY
                next(slabs, None)

        _mix(z_old, xprev_ref, first_of_batch, conv_w_ref, conv_b_ref, w_r_ref, b_r_ref, w_i_ref,
             b_i_ref, lam_ref, w_oa_ref, ln_g_ref, ln_b_ref, b_s_ref, w_ob_ref, w_o_ref, out_ref,
             unperm_ref, w_sp_ref, zero_ref, tail_buf, h_carry, f32_a, f32_b, f32_c, f32_d, f32_e,
             bf_a, bf_b, spent)
        for _ in slabs:
            pass
        _normalise(xnext_ref, g_ref, perm_ref, n_buf, n_perm)

    @pl.when(step % 2 == 0)
    def _():
        body(z_even, z_odd)

    @pl.when(step % 2 == 1)
    def _():
        body(z_odd, z_even)


def _ffn_kernel(h_ref, g_ref, w_up_hbm, w_down_hbm, gf_ref, out_ref, w_up_ref, w_down_ref, stage,
                sems):
    @pl.when(pl.program_id(0) == 0)
    def _():
        _load_as_bf16(w_up_hbm, w_up_ref, stage, sems)
        _load_as_bf16(w_down_hbm, w_down_ref, stage, sems)

    h = h_ref[...]
    n = _rms_norm(h, g_ref[...])
    f = jnp.square(jnp.maximum(_bf16_dot(n, w_up_ref[...]), 0.0))
    h = h + _bf16_dot(f, w_down_ref[...])
    out_ref[...] = _rms_norm(h, gf_ref[...])


def _resident(shape):
    zeros = (0,) * len(shape)
    return pl.BlockSpec(shape, lambda *_: zeros, pipeline_mode=pl.Buffered(1))


_IN_HBM = pl.BlockSpec(memory_space=pl.ANY)


def _mixer(x, g, w_in, conv_w, conv_b, w_r, b_r, w_i, b_i, lam, w_oa, ln_g, ln_b, w_s, b_s, w_ob,
           w_o):
    t, d = x.shape[0] * x.shape[1], x.shape[2]
    tm = MIXER_TILE
    n_tiles = t // tm
    x2 = x.reshape(t, d)
    next_tile = pl.BlockSpec((tm, d), lambda i: (jnp.minimum(i + 1, n_tiles - 1), 0))
    prev_tile = pl.BlockSpec((tm, d), lambda i: (jnp.maximum(i - 1, 0), 0))
    f32, bf16 = jnp.float32, jnp.bfloat16
    params = (g, w_in, conv_w, conv_b, w_r, b_r, w_i, b_i, lam, w_oa, ln_g, ln_b, w_s, b_s, w_ob,
              w_o)
    in_hbm = (w_in, w_r, w_i, w_oa, w_ob, w_o)
    return pl.pallas_call(
        functools.partial(_mixer_kernel, x.shape[1] // tm),
        grid=(n_tiles + 1,),
        in_specs=[next_tile, prev_tile] + [
            _IN_HBM if any(p is w for w in in_hbm) else _resident(p.shape) for p in params],
        out_specs=prev_tile,
        out_shape=jax.ShapeDtypeStruct(x2.shape, x2.dtype),
        scratch_shapes=[pltpu.VMEM((tm, D_IN), f32),
                        pltpu.VMEM((tm, D_IN), f32),
                        pltpu.VMEM((tm, tm), bf16),
                        pltpu.VMEM((tm, tm), bf16),
                        pltpu.VMEM((SGU_GROUPS, tm, tm), bf16),
                        pltpu.VMEM((SUBLANES, LANES), jnp.uint32),
                        pltpu.VMEM((CONV_TAIL, d), f32),
                        pltpu.VMEM((SUBLANES, d), f32),
                        pltpu.VMEM((tm, d), bf16),
                        pltpu.VMEM((tm, d), bf16)]
                       + [pltpu.VMEM((tm, d), f32)] * 5
                       + [pltpu.VMEM((tm, d), bf16)] * 2
                       + [pltpu.VMEM(w.shape, bf16) for w in in_hbm]
                       + [pltpu.VMEM((2, STAGE_ROWS, d), f32),
                          pltpu.VMEM((2, 2 * STAGE_ROWS, LRU_HEAD_DIM), f32),
                          pltpu.SemaphoreType.DMA((2,))],
        compiler_params=pltpu.CompilerParams(
            dimension_semantics=("arbitrary",),
            vmem_limit_bytes=MIXER_VMEM_BYTES),
        name="mixer",
    )(x2, x2, *params)


def _ffn(h, g, w_up, w_down, gf):
    t, d = h.shape
    tm = FFN_TILE
    tile = pl.BlockSpec((tm, d), lambda i: (i, 0))
    return pl.pallas_call(
        _ffn_kernel,
        grid=(t // tm,),
        in_specs=[tile, _resident(g.shape), _IN_HBM, _IN_HBM, _resident(gf.shape)],
        out_specs=tile,
        out_shape=jax.ShapeDtypeStruct(h.shape, h.dtype),
        scratch_shapes=[pltpu.VMEM(w_up.shape, jnp.bfloat16),
                        pltpu.VMEM(w_down.shape, jnp.bfloat16),
                        pltpu.VMEM((2, STAGE_ROWS, d), jnp.float32),
                        pltpu.SemaphoreType.DMA((2,))],
        compiler_params=pltpu.CompilerParams(
            dimension_semantics=("arbitrary",),
            vmem_limit_bytes=FFN_VMEM_BYTES),
        name="ffn",
    )(h, g, w_up, w_down, gf)


def _segment_major_bias(b_s):
    per_time = jnp.tile(jnp.transpose(b_s), (MIXER_TILE // CHUNK, 1))
    return per_time.reshape(SUBLANES, SEG, -1).swapaxes(0, 1).reshape(MIXER_TILE, -1)


def kernel(x, norm_mix_g, w_in, conv_w, conv_b, w_rgate, b_rgate, w_igate, b_igate, lru_lambda, w_out_a, sgu_ln_g, sgu_ln_b, sgu_w_s, sgu_b_s, w_out_b, w_out, norm_mlp_g, w_up, w_down, norm_final_g):
    assert w_in.shape[0] == 1
    bsz, seq, d = x.shape
    row = lambda p: p.reshape(1, -1)
    heads_on_rows = lambda w: w.reshape(d, LRU_HEAD_DIM)
    h = _mixer(
        x, row(norm_mix_g[0]), w_in[0], conv_w[0], row(conv_b[0]),
        heads_on_rows(w_rgate[0]), row(b_rgate[0]), heads_on_rows(w_igate[0]), row(b_igate[0]),
        row(lru_lambda[0]), w_out_a[0], row(sgu_ln_g[0]), row(sgu_ln_b[0]),
        sgu_w_s[0], _segment_major_bias(sgu_b_s[0]), w_out_b[0], w_out[0])
    out = _ffn(h, row(norm_mlp_g[0]), w_up[0], w_down[0], row(norm_final_g))
    return out.reshape(bsz, seq, d)
```

```python
import functools
import math

import jax
import jax.numpy as jnp
from jax import lax
from jax.experimental import pallas as pl
from jax.experimental.pallas import tpu as pltpu

D_MODEL = 1024
LRU_HEADS = 4
LRU_HEAD_DIM = D_MODEL // LRU_HEADS
CONV_WIDTH = 4
LRU_C = 8.0
SGU_GROUPS = 4
SGU_GROUP_DIM = D_MODEL // SGU_GROUPS
CHUNK = 128
D_FF = 4 * D_MODEL
D_IN = 6 * D_MODEL
NORM_EPS = 1e-6
LN_EPS = 1e-5

SUBLANES = 8
LANES = 128
BF16_ROWS = 16

MIXER_TILE = 256
SEG = MIXER_TILE // SUBLANES
CONV_TAIL = (CONV_WIDTH - 1) * SUBLANES
ROW_BLOCK = BF16_ROWS
PROJ_SLAB = 256
FFN_TILE = 512
FFN_STAGE_SLOTS = 4
MIXER_VMEM_BYTES = 56 * 1024 * 1024
FFN_VMEM_BYTES = 48 * 1024 * 1024

_COST_CONV, _COST_GATE, _COST_SCAN, _COST_FIX = 9, 14, 3, 9
_COST_GELU_LN, _COST_GATED, _COST_MERGE = 20, 4, 8
_COST_TOTAL = (_COST_CONV + _COST_GATE + _COST_SCAN + _COST_FIX + _COST_GELU_LN + _COST_GATED
               + _COST_MERGE) * (MIXER_TILE // ROW_BLOCK)
SLAB_EVERY = 0.3 * _COST_TOTAL / (D_IN // PROJ_SLAB)

_LOG2E = math.log2(math.e)
_GELU_C1 = -2.0 * 0.7978845608028654 * _LOG2E
_GELU_C2 = _GELU_C1 * 0.044715


def _gelu(x):
    return x / (1.0 + jnp.exp2(x * (_GELU_C1 + _GELU_C2 * (x * x))))


def _sigmoid(x):
    return 1.0 / (1.0 + jnp.exp2(-_LOG2E * x))


def _rms_norm(x, g):
    ms = jnp.mean(x * x, axis=-1, keepdims=True)
    return x * lax.rsqrt(ms + NORM_EPS) * g


def _bf16_dot(a, w):
    return jnp.dot(a.astype(jnp.bfloat16), w, preferred_element_type=jnp.float32)


def _segment_time(p):
    return (p % SUBLANES) * SEG + p // SUBLANES


def _sublane_shift(x, prev):
    sub = lax.broadcasted_iota(jnp.int32, x.shape, 0) % SUBLANES
    groups = x.shape[0] // SUBLANES
    x3 = x.reshape(groups, SUBLANES, x.shape[1])
    p3 = prev.reshape(groups, SUBLANES, x.shape[1])
    return jnp.where(sub == 0, pltpu.roll(p3, 1, 1).reshape(x.shape),
                     pltpu.roll(x3, 1, 1).reshape(x.shape))


def _row_blocks(tm):
    return [slice(r, r + ROW_BLOCK) for r in range(0, tm, ROW_BLOCK)]


def _once_complete(lhs_ref, last_block, zero_ref):
    bits = pltpu.bitcast(last_block[0:SUBLANES, 0:LANES], jnp.uint32) & zero_ref[...]
    zero = pltpu.bitcast(bits, jnp.float32)
    zero = jnp.concatenate([zero] * (BF16_ROWS // SUBLANES), axis=0).astype(lhs_ref.dtype)
    lhs = lhs_ref[...]
    corner = lhs[0:BF16_ROWS, 0:LANES] + zero
    top = jnp.concatenate([corner, lhs[0:BF16_ROWS, LANES:]], axis=1)
    return jnp.concatenate([top, lhs[BF16_ROWS:, :]], axis=0)


def _load_as_bf16(jobs, slots, sems):
    rows, width = slots[0].shape
    depth = len(slots)
    chunks = []
    for src, dst in jobs:
        k, n = src.shape
        assert k % rows == 0 and n % width == 0
        chunks += [(src, dst, r, c) for c in range(0, n, width) for r in range(0, k, rows)]

    def copy(i):
        src, _, r, c = chunks[i]
        return pltpu.make_async_copy(
            src.at[pl.ds(r, rows), pl.ds(c, width)], slots[i % depth], sems.at[i % depth])

    for i in range(min(depth, len(chunks))):
        copy(i).start()
    for i, (_, dst, r, c) in enumerate(chunks):
        copy(i).wait()
        dst[r:r + rows, c:c + width] = slots[i % depth][...].astype(jnp.bfloat16)
        if i + depth < len(chunks):
            copy(i + depth).start()


def _normalise(x_ref, g_ref, perm_ref, n_buf, n_perm):
    g = g_ref[...]
    for rows in _row_blocks(x_ref.shape[0]):
        n_buf[rows, :] = _rms_norm(x_ref[rows, :], g).astype(jnp.bfloat16)
    n_perm[...] = jnp.dot(perm_ref[...], n_buf[...],
                          preferred_element_type=jnp.float32).astype(jnp.bfloat16)


def _project_slabs(n_perm, w_in_ref, z_ref):
    for k in range(D_IN // PROJ_SLAB):
        cols = slice(k * PROJ_SLAB, (k + 1) * PROJ_SLAB)
        z_ref[:, cols] = jnp.dot(n_perm[...], w_in_ref[:, cols], preferred_element_type=jnp.float32)
        yield


def _mix(z_ref, x_ref, first_of_batch, conv_w_ref, conv_b_ref, w_r_ref, b_r_ref, w_i_ref, b_i_ref,
         lam_ref, w_oa_ref, ln_g_ref, ln_b_ref, b_s_ref, w_ob_ref, w_o_ref, out_ref,
         unperm_ref, w_sp_ref, zero_ref, tail_buf, h_carry, f32_a, f32_b, f32_c, f32_d, f32_e,
         bf_a, bf_b, spent):
    tm = x_ref.shape[0]
    d = D_MODEL
    blocks = _row_blocks(tm)
    col = lambda k: slice(k * d, (k + 1) * d)

    tail = z_ref[tm - CONV_TAIL:tm, col(0)]
    wrapped = _sublane_shift(tail, jnp.where(first_of_batch, 0.0, tail_buf[...]))
    tail_buf[...] = tail
    conv_b = conv_b_ref[...]
    conv_w = [conv_w_ref[k:k + 1, :] for k in range(CONV_WIDTH)]
    for h in range(LRU_HEADS):
        hcols = slice(h * LRU_HEAD_DIM, (h + 1) * LRU_HEAD_DIM)
        for rows in blocks:
            xc = conv_b[:, hcols] + conv_w[0][:, hcols] * z_ref[rows, hcols]
            for k in range(1, CONV_WIDTH):
                lo = rows.start - k * SUBLANES
                pieces = []
                if lo < 0:
                    pieces.append(
                        wrapped[CONV_TAIL + lo:min(CONV_TAIL, CONV_TAIL + lo + ROW_BLOCK), hcols])
                if lo + ROW_BLOCK > 0:
                    pieces.append(z_ref[max(lo, 0):lo + ROW_BLOCK, hcols])
                xc = xc + conv_w[k][:, hcols] * jnp.concatenate(pieces, axis=0)
            f32_c[rows, hcols] = xc
            bf_a[rows, hcols] = xc.astype(jnp.bfloat16)
            spent(_COST_CONV / LRU_HEADS)
        xcb = bf_a[:, hcols]
        f32_a[:, hcols] = jnp.dot(xcb, w_r_ref[hcols, :], preferred_element_type=jnp.float32)
        f32_b[:, hcols] = jnp.dot(xcb, w_i_ref[hcols, :], preferred_element_type=jnp.float32)

    ln_g, ln_b = ln_g_ref[...], ln_b_ref[...]
    for rows in blocks:
        f32_d[rows, :] = _gelu(z_ref[rows, col(2)])
        v = _gelu(z_ref[rows, col(3)])
        vc = v - jnp.mean(v, axis=-1, keepdims=True)
        var = jnp.mean(vc * vc, axis=-1, keepdims=True)
        vn = vc * lax.rsqrt(var + LN_EPS) * ln_g + ln_b
        bf_b[rows, :] = vn.astype(jnp.bfloat16)
        spent(_COST_GELU_LN)

    vnb = _once_complete(bf_b, vn, zero_ref)
    for g in range(SGU_GROUPS):
        gcols = slice(g * SGU_GROUP_DIM, (g + 1) * SGU_GROUP_DIM)
        f32_e[:, gcols] = jnp.dot(w_sp_ref[g], vnb[:, gcols], preferred_element_type=jnp.float32)

    neg_lam = -lam_ref[...]
    softplus = jnp.maximum(neg_lam, 0.0) + jnp.log1p(jnp.exp(-jnp.abs(neg_lam)))
    log_a_scale = -LRU_C * softplus
    b_r, b_i = b_r_ref[...], b_i_ref[...]
    for rows in blocks:
        log_a = log_a_scale * _sigmoid(f32_a[rows, :] + b_r)
        t = jnp.tanh(log_a)
        q = -2.0 * t / (1.0 - t)
        mult = jnp.where(q > 0.0, q * lax.rsqrt(q), 0.0)
        f32_b[rows, :] = f32_c[rows, :] * _sigmoid(f32_b[rows, :] + b_i) * mult
        f32_a[rows, :] = jnp.exp(log_a)
        spent(_COST_GATE)

    for rows in blocks:
        bias = jnp.concatenate(
            [jnp.broadcast_to(b_s_ref[rows, g:g + 1], (ROW_BLOCK, SGU_GROUP_DIM))
             for g in range(SGU_GROUPS)], axis=1)
        yb = f32_d[rows, :] * (f32_e[rows, :] + bias)
        bf_a[rows, :] = yb.astype(jnp.bfloat16)
        spent(_COST_GATED)
    f32_c[...] = jnp.dot(_once_complete(bf_a, yb, zero_ref), w_ob_ref[...],
                         preferred_element_type=jnp.float32)

    grp = lambda j: slice(j * SUBLANES, (j + 1) * SUBLANES)
    h_run, a_run = f32_b[grp(0), :], f32_a[grp(0), :]
    for j in range(1, SEG):
        a_j = f32_a[grp(j), :]
        h_run = a_j * h_run + f32_b[grp(j), :]
        a_run = a_j * a_run
        f32_b[grp(j), :] = h_run
        f32_a[grp(j), :] = a_run
        if j % (ROW_BLOCK // SUBLANES) == 0:
            spent(_COST_SCAN)
    sub = lax.broadcasted_iota(jnp.int32, (SUBLANES, d), 0)
    shift = 1
    while shift < SUBLANES:
        keep = sub >= shift
        a_prev = jnp.where(keep, pltpu.roll(a_run, shift, 0), 1.0)
        h_prev = jnp.where(keep, pltpu.roll(h_run, shift, 0), 0.0)
        h_run = a_run * h_prev + h_run
        a_run = a_run * a_prev
        shift *= 2
    carry = jnp.where(first_of_batch, 0.0, h_carry[...])
    seg_end = a_run * carry + h_run
    seg_start = jnp.where(sub == 0, carry, pltpu.roll(seg_end, 1, 0))
    h_carry[...] = jnp.broadcast_to(seg_end[SUBLANES - 1:SUBLANES, :], (SUBLANES, d))
    seg_start = jnp.concatenate([seg_start] * (ROW_BLOCK // SUBLANES), axis=0)
    for rows in blocks:
        h = f32_b[rows, :] + f32_a[rows, :] * seg_start
        ya = h * _gelu(z_ref[rows, col(1)])
        bf_b[rows, :] = ya.astype(jnp.bfloat16)
        spent(_COST_FIX)
    f32_e[...] = jnp.dot(_once_complete(bf_b, ya, zero_ref), w_oa_ref[...],
                         preferred_element_type=jnp.float32)

    for rows in blocks:
        merged = (_sigmoid(z_ref[rows, col(4)]) * f32_e[rows, :]
                  + _sigmoid(z_ref[rows, col(5)]) * f32_c[rows, :])
        bf_a[rows, :] = merged.astype(jnp.bfloat16)
        spent(_COST_MERGE)
    merged = jnp.dot(unperm_ref[...], _once_complete(bf_a, merged, zero_ref),
                     preferred_element_type=jnp.float32)
    out_ref[...] = x_ref[...] + _bf16_dot(merged, w_o_ref[...])


def _setup_permutations(w_s_ref, perm_ref, unperm_ref, w_sp_ref):
    tm = perm_ref.shape[0]
    rows = lax.broadcasted_iota(jnp.int32, (tm, tm), 0)
    cols = lax.broadcasted_iota(jnp.int32, (tm, tm), 1)
    perm = (cols == _segment_time(rows)).astype(jnp.bfloat16)
    unperm = (rows == _segment_time(cols)).astype(jnp.bfloat16)
    perm_ref[...] = perm
    unperm_ref[...] = unperm
    mixes_chunk = (rows // CHUNK == cols // CHUNK) & (cols <= rows)
    reps = tm // CHUNK
    for g in range(SGU_GROUPS):
        w_tiled = jnp.concatenate([jnp.concatenate([w_s_ref[g]] * reps, axis=1)] * reps, axis=0)
        w_time = jnp.where(mixes_chunk, w_tiled, 0.0).astype(jnp.bfloat16)
        w_rows = jnp.dot(perm, w_time, preferred_element_type=jnp.float32).astype(jnp.bfloat16)
        w_sp_ref[g] = jnp.dot(w_rows, unperm, preferred_element_type=jnp.float32).astype(jnp.bfloat16)


def _mixer_kernel(tiles_per_batch, xnext_ref, xprev_ref, g_ref, w_in_hbm, *rest):
    (conv_w_ref, conv_b_ref, w_r_hbm, b_r_ref, w_i_hbm, b_i_ref, lam_ref, w_oa_hbm, ln_g_ref,
     ln_b_ref, w_s_ref, b_s_ref, w_ob_hbm, w_o_hbm, out_ref,
     z_even, z_odd, perm_ref, unperm_ref, w_sp_ref, zero_ref, tail_buf, h_carry,
     n_buf, n_perm, f32_a, f32_b, f32_c, f32_d, f32_e, bf_a, bf_b,
     w_in_ref, w_r_ref, w_i_ref, w_oa_ref, w_ob_ref, w_o_ref, gate_stage, sems) = rest
    step = pl.program_id(0)
    first_of_batch = (step - 1) % tiles_per_batch == 0

    @pl.when(step == 0)
    def _():
        _load_as_bf16([(w_in_hbm, w_in_ref)], [z_even, z_odd], sems)
        _load_as_bf16([(w_oa_hbm, w_oa_ref), (w_ob_hbm, w_ob_ref), (w_o_hbm, w_o_ref)],
                      [f32_a, f32_b, f32_c, f32_d, f32_e], sems)
        _load_as_bf16([(w_r_hbm, w_r_ref), (w_i_hbm, w_i_ref)], [gate_stage.at[0], gate_stage.at[1]],
                      sems)
        _setup_permutations(w_s_ref, perm_ref, unperm_ref, w_sp_ref)
        z_odd[...] = jnp.zeros_like(z_odd)
        zero_ref[...] = jnp.zeros_like(zero_ref)
        tail_buf[...] = jnp.zeros_like(tail_buf)
        h_carry[...] = jnp.zeros_like(h_carry)
        _normalise(xprev_ref, g_ref, perm_ref, n_buf, n_perm)

    def body(z_new, z_old):
        slabs = _project_slabs(n_perm, w_in_ref, z_new)
        meter = [0.0, 0.0]

        def spent(cost):
            meter[0] += cost
            while meter[0] >= meter[1]:
                meter[1] += SLAB_EVERY
                next(slabs, None)

        _mix(z_old, xprev_ref, first_of_batch, conv_w_ref, conv_b_ref, w_r_ref, b_r_ref, w_i_ref,
             b_i_ref, lam_ref, w_oa_ref, ln_g_ref, ln_b_ref, b_s_ref, w_ob_ref, w_o_ref, out_ref,
             unperm_ref, w_sp_ref, zero_ref, tail_buf, h_carry, f32_a, f32_b, f32_c, f32_d, f32_e,
             bf_a, bf_b, spent)
        for _ in slabs:
            pass
        _normalise(xnext_ref, g_ref, perm_ref, n_buf, n_perm)

    @pl.when(step % 2 == 0)
    def _():
        body(z_even, z_odd)

    @pl.when(step % 2 == 1)
    def _():
        body(z_odd, z_even)


def _ffn_kernel(h_ref, g_ref, w_up_hbm, w_down_hbm, gf_ref, out_ref, w_up_ref, w_down_ref, stage,
                sems):
    @pl.when(pl.program_id(0) == 0)
    def _():
        _load_as_bf16([(w_up_hbm, w_up_ref), (w_down_hbm, w_down_ref)],
                      [stage.at[i] for i in range(stage.shape[0])], sems)

    h = h_ref[...]
    n = _rms_norm(h, g_ref[...])
    f = jnp.square(jnp.maximum(_bf16_dot(n, w_up_ref[...]), 0.0))
    h = h + _bf16_dot(f, w_down_ref[...])
    out_ref[...] = _rms_norm(h, gf_ref[...])


def _resident(shape):
    zeros = (0,) * len(shape)
    return pl.BlockSpec(shape, lambda *_: zeros, pipeline_mode=pl.Buffered(1))


_IN_HBM = pl.BlockSpec(memory_space=pl.ANY)


def _mixer(x, g, w_in, conv_w, conv_b, w_r, b_r, w_i, b_i, lam, w_oa, ln_g, ln_b, w_s, b_s, w_ob,
           w_o):
    t, d = x.shape[0] * x.shape[1], x.shape[2]
    tm = MIXER_TILE
    n_tiles = t // tm
    x2 = x.reshape(t, d)
    next_tile = pl.BlockSpec((tm, d), lambda i: (jnp.minimum(i + 1, n_tiles - 1), 0))
    prev_tile = pl.BlockSpec((tm, d), lambda i: (jnp.maximum(i - 1, 0), 0))
    f32, bf16 = jnp.float32, jnp.bfloat16
    params = (g, w_in, conv_w, conv_b, w_r, b_r, w_i, b_i, lam, w_oa, ln_g, ln_b, w_s, b_s, w_ob,
              w_o)
    in_hbm = (w_in, w_r, w_i, w_oa, w_ob, w_o)
    return pl.pallas_call(
        functools.partial(_mixer_kernel, x.shape[1] // tm),
        grid=(n_tiles + 1,),
        in_specs=[next_tile, prev_tile] + [
            _IN_HBM if any(p is w for w in in_hbm) else _resident(p.shape) for p in params],
        out_specs=prev_tile,
        out_shape=jax.ShapeDtypeStruct(x2.shape, x2.dtype),
        scratch_shapes=[pltpu.VMEM((tm, D_IN), f32),
                        pltpu.VMEM((tm, D_IN), f32),
                        pltpu.VMEM((tm, tm), bf16),
                        pltpu.VMEM((tm, tm), bf16),
                        pltpu.VMEM((SGU_GROUPS, tm, tm), bf16),
                        pltpu.VMEM((SUBLANES, LANES), jnp.uint32),
                        pltpu.VMEM((CONV_TAIL, d), f32),
                        pltpu.VMEM((SUBLANES, d), f32),
                        pltpu.VMEM((tm, d), bf16),
                        pltpu.VMEM((tm, d), bf16)]
                       + [pltpu.VMEM((tm, d), f32)] * 5
                       + [pltpu.VMEM((tm, d), bf16)] * 2
                       + [pltpu.VMEM(w.shape, bf16) for w in in_hbm]
                       + [pltpu.VMEM((2, d // 2, LRU_HEAD_DIM), f32),
                          pltpu.SemaphoreType.DMA((5,))],
        compiler_params=pltpu.CompilerParams(
            dimension_semantics=("arbitrary",),
            vmem_limit_bytes=MIXER_VMEM_BYTES),
        name="mixer",
    )(x2, x2, *params)


def _ffn(h, g, w_up, w_down, gf):
    t, d = h.shape
    tm = FFN_TILE
    tile = pl.BlockSpec((tm, d), lambda i: (i, 0))
    return pl.pallas_call(
        _ffn_kernel,
        grid=(t // tm,),
        in_specs=[tile, _resident(g.shape), _IN_HBM, _IN_HBM, _resident(gf.shape)],
        out_specs=tile,
        out_shape=jax.ShapeDtypeStruct(h.shape, h.dtype),
        scratch_shapes=[pltpu.VMEM(w_up.shape, jnp.bfloat16),
                        pltpu.VMEM(w_down.shape, jnp.bfloat16),
                        pltpu.VMEM((FFN_STAGE_SLOTS, FFN_TILE, d), jnp.float32),
                        pltpu.SemaphoreType.DMA((FFN_STAGE_SLOTS,))],
        compiler_params=pltpu.CompilerParams(
            dimension_semantics=("arbitrary",),
            vmem_limit_bytes=FFN_VMEM_BYTES),
        name="ffn",
    )(h, g, w_up, w_down, gf)


def _segment_major_bias(b_s):
    per_time = jnp.tile(jnp.transpose(b_s), (MIXER_TILE // CHUNK, 1))
    return per_time.reshape(SUBLANES, SEG, -1).swapaxes(0, 1).reshape(MIXER_TILE, -1)


def kernel(x, norm_mix_g, w_in, conv_w, conv_b, w_rgate, b_rgate, w_igate, b_igate, lru_lambda, w_out_a, sgu_ln_g, sgu_ln_b, sgu_w_s, sgu_b_s, w_out_b, w_out, norm_mlp_g, w_up, w_down, norm_final_g):
    assert w_in.shape[0] == 1
    bsz, seq, d = x.shape
    row = lambda p: p.reshape(1, -1)
    heads_on_rows = lambda w: w.reshape(d, LRU_HEAD_DIM)
    h = _mixer(
        x, row(norm_mix_g[0]), w_in[0], conv_w[0], row(conv_b[0]),
        heads_on_rows(w_rgate[0]), row(b_rgate[0]), heads_on_rows(w_igate[0]), row(b_igate[0]),
        row(lru_lambda[0]), w_out_a[0], row(sgu_ln_g[0]), row(sgu_ln_b[0]),
        sgu_w_s[0], _segment_major_bias(sgu_b_s[0]), w_out_b[0], w_out[0])
    out = _ffn(h, row(norm_mlp_g[0]), w_up[0], w_down[0], row(norm_final_g))
    return out.reshape(bsz, seq, d)
```

```python
import functools
import math

import jax
import jax.numpy as jnp
from jax import lax
from jax.experimental import pallas as pl
from jax.experimental.pallas import tpu as pltpu

D_MODEL = 1024
LRU_HEADS = 4
LRU_HEAD_DIM = D_MODEL // LRU_HEADS
CONV_WIDTH = 4
LRU_C = 8.0
SGU_GROUPS = 4
SGU_GROUP_DIM = D_MODEL // SGU_GROUPS
CHUNK = 128
D_FF = 4 * D_MODEL
D_IN = 6 * D_MODEL
NORM_EPS = 1e-6
LN_EPS = 1e-5

SUBLANES = 8
LANES = 128
BF16_ROWS = 16

MIXER_TILE = 256
SEG = MIXER_TILE // SUBLANES
CONV_TAIL = (CONV_WIDTH - 1) * SUBLANES
ROW_BLOCK = BF16_ROWS
PROJ_SLAB = 256
FFN_TILE = 1024
FFN_CHUNK = 1024
FFN_STAGE_SLOTS = 4
FFN_STAGE_ROWS = 512
MIXER_VMEM_BYTES = 56 * 1024 * 1024
FFN_VMEM_BYTES = 58 * 1024 * 1024

_COST_CONV, _COST_GATE, _COST_SCAN, _COST_FIX = 9, 14, 3, 9
_COST_GELU_LN, _COST_GATED, _COST_MERGE = 20, 4, 8
_COST_TOTAL = (_COST_CONV + _COST_GATE + _COST_SCAN + _COST_FIX + _COST_GELU_LN + _COST_GATED
               + _COST_MERGE) * (MIXER_TILE // ROW_BLOCK)
SLAB_EVERY = 0.3 * _COST_TOTAL / (D_IN // PROJ_SLAB)

_GELU_C1 = math.sqrt(2.0 / math.pi)
_GELU_C2 = _GELU_C1 * 0.044715


def _gelu(x):
    half = 0.5 * x
    return half + half * jnp.tanh(x * (_GELU_C1 + _GELU_C2 * (x * x)))


def _sigmoid(x):
    return 0.5 + 0.5 * jnp.tanh(0.5 * x)


def _rms_norm(x, g):
    ms = jnp.mean(x * x, axis=-1, keepdims=True)
    return x * lax.rsqrt(ms + NORM_EPS) * g


def _bf16_dot(a, w):
    return jnp.dot(a.astype(jnp.bfloat16), w, preferred_element_type=jnp.float32)


def _segment_time(p):
    return (p % SUBLANES) * SEG + p // SUBLANES


def _sublane_shift(x, prev):
    sub = lax.broadcasted_iota(jnp.int32, x.shape, 0) % SUBLANES
    groups = x.shape[0] // SUBLANES
    x3 = x.reshape(groups, SUBLANES, x.shape[1])
    p3 = prev.reshape(groups, SUBLANES, x.shape[1])
    return jnp.where(sub == 0, pltpu.roll(p3, 1, 1).reshape(x.shape),
                     pltpu.roll(x3, 1, 1).reshape(x.shape))


def _row_blocks(tm):
    return [slice(r, r + ROW_BLOCK) for r in range(0, tm, ROW_BLOCK)]


def _once_complete(lhs_ref, last_block, zero_ref):
    bits = pltpu.bitcast(last_block[0:SUBLANES, 0:LANES], jnp.uint32) & zero_ref[...]
    zero = pltpu.bitcast(bits, jnp.float32)
    zero = jnp.concatenate([zero] * (BF16_ROWS // SUBLANES), axis=0).astype(lhs_ref.dtype)
    lhs = lhs_ref[...]
    corner = lhs[0:BF16_ROWS, 0:LANES] + zero
    top = jnp.concatenate([corner, lhs[0:BF16_ROWS, LANES:]], axis=1)
    return jnp.concatenate([top, lhs[BF16_ROWS:, :]], axis=0)


def _load_as_bf16(jobs, slots, sems):
    rows, width = slots[0].shape
    depth = len(slots)
    chunks = []
    for src, dst in jobs:
        k, n = src.shape
        assert k % rows == 0 and n % width == 0
        chunks += [(src, dst, r, c) for c in range(0, n, width) for r in range(0, k, rows)]

    def copy(i):
        src, _, r, c = chunks[i]
        return pltpu.make_async_copy(
            src.at[pl.ds(r, rows), pl.ds(c, width)], slots[i % depth], sems.at[i % depth])

    for i in range(min(depth, len(chunks))):
        copy(i).start()
    for i, (_, dst, r, c) in enumerate(chunks):
        copy(i).wait()
        dst[r:r + rows, c:c + width] = slots[i % depth][...].astype(jnp.bfloat16)
        if i + depth < len(chunks):
            copy(i + depth).start()


def _normalise(x_ref, g_ref, perm_ref, n_buf, n_perm):
    g = g_ref[...]
    for rows in _row_blocks(x_ref.shape[0]):
        n_buf[rows, :] = _rms_norm(x_ref[rows, :], g).astype(jnp.bfloat16)
    n_perm[...] = jnp.dot(perm_ref[...], n_buf[...],
                          preferred_element_type=jnp.float32).astype(jnp.bfloat16)


def _project_slabs(n_perm, w_in_ref, z_ref):
    for k in range(D_IN // PROJ_SLAB):
        cols = slice(k * PROJ_SLAB, (k + 1) * PROJ_SLAB)
        z_ref[:, cols] = jnp.dot(n_perm[...], w_in_ref[:, cols], preferred_element_type=jnp.float32)
        yield


def _mix(z_ref, x_ref, first_of_batch, conv_w_ref, conv_b_ref, w_r_ref, b_r_ref, w_i_ref, b_i_ref,
         lam_ref, w_oa_ref, ln_g_ref, ln_b_ref, b_s_ref, w_ob_ref, w_o_ref, out_ref,
         unperm_ref, w_sp_ref, zero_ref, tail_buf, h_carry, f32_a, f32_b, f32_c, f32_d, f32_e,
         bf_a, bf_b, spent):
    tm = x_ref.shape[0]
    d = D_MODEL
    blocks = _row_blocks(tm)
    col = lambda k: slice(k * d, (k + 1) * d)

    tail = z_ref[tm - CONV_TAIL:tm, col(0)]
    wrapped = _sublane_shift(tail, jnp.where(first_of_batch, 0.0, tail_buf[...]))
    tail_buf[...] = tail
    conv_b = conv_b_ref[...]
    conv_w = [conv_w_ref[k:k + 1, :] for k in range(CONV_WIDTH)]
    for h in range(LRU_HEADS):
        hcols = slice(h * LRU_HEAD_DIM, (h + 1) * LRU_HEAD_DIM)
        for rows in blocks:
            xc = conv_b[:, hcols] + conv_w[0][:, hcols] * z_ref[rows, hcols]
            for k in range(1, CONV_WIDTH):
                lo = rows.start - k * SUBLANES
                pieces = []
                if lo < 0:
                    pieces.append(
                        wrapped[CONV_TAIL + lo:min(CONV_TAIL, CONV_TAIL + lo + ROW_BLOCK), hcols])
                if lo + ROW_BLOCK > 0:
                    pieces.append(z_ref[max(lo, 0):lo + ROW_BLOCK, hcols])
                xc = xc + conv_w[k][:, hcols] * jnp.concatenate(pieces, axis=0)
            f32_c[rows, hcols] = xc
            bf_a[rows, hcols] = xc.astype(jnp.bfloat16)
            spent(_COST_CONV / LRU_HEADS)
        xcb = bf_a[:, hcols]
        f32_a[:, hcols] = jnp.dot(xcb, w_r_ref[hcols, :], preferred_element_type=jnp.float32)
        f32_b[:, hcols] = jnp.dot(xcb, w_i_ref[hcols, :], preferred_element_type=jnp.float32)

    ln_g, ln_b = ln_g_ref[...], ln_b_ref[...]
    for rows in blocks:
        f32_d[rows, :] = _gelu(z_ref[rows, col(2)])
        v = _gelu(z_ref[rows, col(3)])
        vc = v - jnp.mean(v, axis=-1, keepdims=True)
        var = jnp.mean(vc * vc, axis=-1, keepdims=True)
        vn = vc * lax.rsqrt(var + LN_EPS) * ln_g + ln_b
        bf_b[rows, :] = vn.astype(jnp.bfloat16)
        spent(_COST_GELU_LN)

    vnb = _once_complete(bf_b, vn, zero_ref)
    for g in range(SGU_GROUPS):
        gcols = slice(g * SGU_GROUP_DIM, (g + 1) * SGU_GROUP_DIM)
        f32_e[:, gcols] = jnp.dot(w_sp_ref[g], vnb[:, gcols], preferred_element_type=jnp.float32)

    neg_lam = -lam_ref[...]
    softplus = jnp.maximum(neg_lam, 0.0) + jnp.log1p(jnp.exp(-jnp.abs(neg_lam)))
    log_a_scale = -LRU_C * softplus
    b_r, b_i = b_r_ref[...], b_i_ref[...]
    for rows in blocks:
        log_a = log_a_scale * _sigmoid(f32_a[rows, :] + b_r)
        t = jnp.tanh(log_a)
        q = -2.0 * t / (1.0 - t)
        mult = jnp.where(q > 0.0, q * lax.rsqrt(q), 0.0)
        f32_b[rows, :] = f32_c[rows, :] * _sigmoid(f32_b[rows, :] + b_i) * mult
        f32_a[rows, :] = jnp.exp(log_a)
        spent(_COST_GATE)

    for rows in blocks:
        bias = jnp.concatenate(
            [jnp.broadcast_to(b_s_ref[rows, g:g + 1], (ROW_BLOCK, SGU_GROUP_DIM))
             for g in range(SGU_GROUPS)], axis=1)
        yb = f32_d[rows, :] * (f32_e[rows, :] + bias)
        bf_a[rows, :] = yb.astype(jnp.bfloat16)
        spent(_COST_GATED)
    f32_c[...] = jnp.dot(_once_complete(bf_a, yb, zero_ref), w_ob_ref[...],
                         preferred_element_type=jnp.float32)

    grp = lambda j: slice(j * SUBLANES, (j + 1) * SUBLANES)
    h_run, a_run = f32_b[grp(0), :], f32_a[grp(0), :]
    for j in range(1, SEG):
        a_j = f32_a[grp(j), :]
        h_run = a_j * h_run + f32_b[grp(j), :]
        a_run = a_j * a_run
        f32_b[grp(j), :] = h_run
        f32_a[grp(j), :] = a_run
        if j % (ROW_BLOCK // SUBLANES) == 0:
            spent(_COST_SCAN)
    sub = lax.broadcasted_iota(jnp.int32, (SUBLANES, d), 0)
    shift = 1
    while shift < SUBLANES:
        keep = sub >= shift
        a_prev = jnp.where(keep, pltpu.roll(a_run, shift, 0), 1.0)
        h_prev = jnp.where(keep, pltpu.roll(h_run, shift, 0), 0.0)
        h_run = a_run * h_prev + h_run
        a_run = a_run * a_prev
        shift *= 2
    carry = jnp.where(first_of_batch, 0.0, h_carry[...])
    seg_end = a_run * carry + h_run
    seg_start = jnp.where(sub == 0, carry, pltpu.roll(seg_end, 1, 0))
    h_carry[...] = jnp.broadcast_to(seg_end[SUBLANES - 1:SUBLANES, :], (SUBLANES, d))
    seg_start = jnp.concatenate([seg_start] * (ROW_BLOCK // SUBLANES), axis=0)
    for rows in blocks:
        h = f32_b[rows, :] + f32_a[rows, :] * seg_start
        ya = h * _gelu(z_ref[rows, col(1)])
        bf_b[rows, :] = ya.astype(jnp.bfloat16)
        spent(_COST_FIX)
    f32_e[...] = jnp.dot(_once_complete(bf_b, ya, zero_ref), w_oa_ref[...],
                         preferred_element_type=jnp.float32)

    for rows in blocks:
        merged = (_sigmoid(z_ref[rows, col(4)]) * f32_e[rows, :]
                  + _sigmoid(z_ref[rows, col(5)]) * f32_c[rows, :])
        bf_a[rows, :] = merged.astype(jnp.bfloat16)
        spent(_COST_MERGE)
    merged = jnp.dot(unperm_ref[...], _once_complete(bf_a, merged, zero_ref),
                     preferred_element_type=jnp.float32)
    out_ref[...] = x_ref[...] + _bf16_dot(merged, w_o_ref[...])


def _setup_permutations(w_s_ref, perm_ref, unperm_ref, w_sp_ref):
    tm = perm_ref.shape[0]
    rows = lax.broadcasted_iota(jnp.int32, (tm, tm), 0)
    cols = lax.broadcasted_iota(jnp.int32, (tm, tm), 1)
    perm = (cols == _segment_time(rows)).astype(jnp.bfloat16)
    unperm = (rows == _segment_time(cols)).astype(jnp.bfloat16)
    perm_ref[...] = perm
    unperm_ref[...] = unperm
    mixes_chunk = (rows // CHUNK == cols // CHUNK) & (cols <= rows)
    reps = tm // CHUNK
    for g in range(SGU_GROUPS):
        w_tiled = jnp.concatenate([jnp.concatenate([w_s_ref[g]] * reps, axis=1)] * reps, axis=0)
        w_time = jnp.where(mixes_chunk, w_tiled, 0.0).astype(jnp.bfloat16)
        w_rows = jnp.dot(perm, w_time, preferred_element_type=jnp.float32).astype(jnp.bfloat16)
        w_sp_ref[g] = jnp.dot(w_rows, unperm, preferred_element_type=jnp.float32).astype(jnp.bfloat16)


def _mixer_kernel(tiles_per_batch, xnext_ref, xprev_ref, g_ref, w_in_hbm, *rest):
    (conv_w_ref, conv_b_ref, w_r_hbm, b_r_ref, w_i_hbm, b_i_ref, lam_ref, w_oa_hbm, ln_g_ref,
     ln_b_ref, w_s_ref, b_s_ref, w_ob_hbm, w_o_hbm, out_ref,
     z_even, z_odd, perm_ref, unperm_ref, w_sp_ref, zero_ref, tail_buf, h_carry,
     n_buf, n_perm, f32_a, f32_b, f32_c, f32_d, f32_e, bf_a, bf_b,
     w_in_ref, w_r_ref, w_i_ref, w_oa_ref, w_ob_ref, w_o_ref, gate_stage, sems) = rest
    step = pl.program_id(0)
    first_of_batch = (step - 1) % tiles_per_batch == 0

    @pl.when(step == 0)
    def _():
        _load_as_bf16([(w_in_hbm, w_in_ref)], [z_even, z_odd], sems)
        _load_as_bf16([(w_oa_hbm, w_oa_ref), (w_ob_hbm, w_ob_ref), (w_o_hbm, w_o_ref)],
                      [f32_a, f32_b, f32_c, f32_d, f32_e], sems)
        _load_as_bf16([(w_r_hbm, w_r_ref), (w_i_hbm, w_i_ref)], [gate_stage.at[0], gate_stage.at[1]],
                      sems)
        _setup_permutations(w_s_ref, perm_ref, unperm_ref, w_sp_ref)
        z_odd[...] = jnp.zeros_like(z_odd)
        zero_ref[...] = jnp.zeros_like(zero_ref)
        tail_buf[...] = jnp.zeros_like(tail_buf)
        h_carry[...] = jnp.zeros_like(h_carry)
        _normalise(xprev_ref, g_ref, perm_ref, n_buf, n_perm)

    def body(z_new, z_old):
        slabs = _project_slabs(n_perm, w_in_ref, z_new)
        meter = [0.0, 0.0]

        def spent(cost):
            meter[0] += cost
            while meter[0] >= meter[1]:
                meter[1] += SLAB_EVERY
                next(slabs, None)

        _mix(z_old, xprev_ref, first_of_batch, conv_w_ref, conv_b_ref, w_r_ref, b_r_ref, w_i_ref,
             b_i_ref, lam_ref, w_oa_ref, ln_g_ref, ln_b_ref, b_s_ref, w_ob_ref, w_o_ref, out_ref,
             unperm_ref, w_sp_ref, zero_ref, tail_buf, h_carry, f32_a, f32_b, f32_c, f32_d, f32_e,
             bf_a, bf_b, spent)
        for _ in slabs:
            pass
        _normalise(xnext_ref, g_ref, perm_ref, n_buf, n_perm)

    @pl.when(step % 2 == 0)
    def _():
        body(z_even, z_odd)

    @pl.when(step % 2 == 1)
    def _():
        body(z_odd, z_even)


def _ffn_kernel(h_ref, g_ref, w_up_hbm, w_down_hbm, gf_ref, out_ref, w_up_ref, w_down_ref, stage,
                sems):
    @pl.when(pl.program_id(0) == 0)
    def _():
        _load_as_bf16([(w_up_hbm, w_up_ref), (w_down_hbm, w_down_ref)],
                      [stage.at[i] for i in range(stage.shape[0])], sems)

    h = h_ref[...]
    n = _rms_norm(h, g_ref[...]).astype(jnp.bfloat16)
    for c in range(0, D_FF, FFN_CHUNK):
        f = jnp.dot(n, w_up_ref[:, c:c + FFN_CHUNK], preferred_element_type=jnp.float32)
        f = jnp.square(jnp.maximum(f, 0.0))
        h = h + _bf16_dot(f, w_down_ref[c:c + FFN_CHUNK, :])
    out_ref[...] = _rms_norm(h, gf_ref[...])


def _resident(shape):
    zeros = (0,) * len(shape)
    return pl.BlockSpec(shape, lambda *_: zeros, pipeline_mode=pl.Buffered(1))


_IN_HBM = pl.BlockSpec(memory_space=pl.ANY)


def _mixer(x, g, w_in, conv_w, conv_b, w_r, b_r, w_i, b_i, lam, w_oa, ln_g, ln_b, w_s, b_s, w_ob,
           w_o):
    t, d = x.shape[0] * x.shape[1], x.shape[2]
    tm = MIXER_TILE
    n_tiles = t // tm
    x2 = x.reshape(t, d)
    next_tile = pl.BlockSpec((tm, d), lambda i: (jnp.minimum(i + 1, n_tiles - 1), 0))
    prev_tile = pl.BlockSpec((tm, d), lambda i: (jnp.maximum(i - 1, 0), 0))
    f32, bf16 = jnp.float32, jnp.bfloat16
    params = (g, w_in, conv_w, conv_b, w_r, b_r, w_i, b_i, lam, w_oa, ln_g, ln_b, w_s, b_s, w_ob,
              w_o)
    in_hbm = (w_in, w_r, w_i, w_oa, w_ob, w_o)
    return pl.pallas_call(
        functools.partial(_mixer_kernel, x.shape[1] // tm),
        grid=(n_tiles + 1,),
        in_specs=[next_tile, prev_tile] + [
            _IN_HBM if any(p is w for w in in_hbm) else _resident(p.shape) for p in params],
        out_specs=prev_tile,
        out_shape=jax.ShapeDtypeStruct(x2.shape, x2.dtype),
        scratch_shapes=[pltpu.VMEM((tm, D_IN), f32),
                        pltpu.VMEM((tm, D_IN), f32),
                        pltpu.VMEM((tm, tm), bf16),
                        pltpu.VMEM((tm, tm), bf16),
                        pltpu.VMEM((SGU_GROUPS, tm, tm), bf16),
                        pltpu.VMEM((SUBLANES, LANES), jnp.uint32),
                        pltpu.VMEM((CONV_TAIL, d), f32),
                        pltpu.VMEM((SUBLANES, d), f32),
                        pltpu.VMEM((tm, d), bf16),
                        pltpu.VMEM((tm, d), bf16)]
                       + [pltpu.VMEM((tm, d), f32)] * 5
                       + [pltpu.VMEM((tm, d), bf16)] * 2
                       + [pltpu.VMEM(w.shape, bf16) for w in in_hbm]
                       + [pltpu.VMEM((2, d // 2, LRU_HEAD_DIM), f32),
                          pltpu.SemaphoreType.DMA((5,))],
        compiler_params=pltpu.CompilerParams(
            dimension_semantics=("arbitrary",),
            vmem_limit_bytes=MIXER_VMEM_BYTES),
        name="mixer",
    )(x2, x2, *params)


def _ffn(h, g, w_up, w_down, gf):
    t, d = h.shape
    tm = FFN_TILE
    tile = pl.BlockSpec((tm, d), lambda i: (i, 0))
    return pl.pallas_call(
        _ffn_kernel,
        grid=(t // tm,),
        in_specs=[tile, _resident(g.shape), _IN_HBM, _IN_HBM, _resident(gf.shape)],
        out_specs=tile,
        out_shape=jax.ShapeDtypeStruct(h.shape, h.dtype),
        scratch_shapes=[pltpu.VMEM(w_up.shape, jnp.bfloat16),
                        pltpu.VMEM(w_down.shape, jnp.bfloat16),
                        pltpu.VMEM((FFN_STAGE_SLOTS, FFN_STAGE_ROWS, d), jnp.float32),
                        pltpu.SemaphoreType.DMA((FFN_STAGE_SLOTS,))],
        compiler_params=pltpu.CompilerParams(
            dimension_semantics=("arbitrary",),
            vmem_limit_bytes=FFN_VMEM_BYTES),
        name="ffn",
    )(h, g, w_up, w_down, gf)


def _segment_major_bias(b_s):
    per_time = jnp.tile(jnp.transpose(b_s), (MIXER_TILE // CHUNK, 1))
    return per_time.reshape(SUBLANES, SEG, -1).swapaxes(0, 1).reshape(MIXER_TILE, -1)


def kernel(x, norm_mix_g, w_in, conv_w, conv_b, w_rgate, b_rgate, w_igate, b_igate, lru_lambda, w_out_a, sgu_ln_g, sgu_ln_b, sgu_w_s, sgu_b_s, w_out_b, w_out, norm_mlp_g, w_up, w_down, norm_final_g):
    assert w_in.shape[0] == 1
    bsz, seq, d = x.shape
    row = lambda p: p.reshape(1, -1)
    heads_on_rows = lambda w: w.reshape(d, LRU_HEAD_DIM)
    h = _mixer(
        x, row(norm_mix_g[0]), w_in[0], conv_w[0], row(conv_b[0]),
        heads_on_rows(w_rgate[0]), row(b_rgate[0]), heads_on_rows(w_igate[0]), row(b_igate[0]),
        row(lru_lambda[0]), w_out_a[0], row(sgu_ln_g[0]), row(sgu_ln_b[0]),
        sgu_w_s[0], _segment_major_bias(sgu_b_s[0]), w_out_b[0], w_out[0])
    out = _ffn(h, row(norm_mlp_g[0]), w_up[0], w_down[0], row(norm_final_g))
    return out.reshape(bsz, seq, d)
```
